```python
import math
import jax
import jax.numpy as jnp
from jax import lax
import numpy as np

D_MODEL = 1024
BATCH = 32
SEQ = 256
DEPTH = 4
DEC_BATCH = 4
DEC_SEQ = 1024
PAST_LEN = 256

GRID_W = 64
N_MIXERS = 3
N_ATTN = (DEPTH + 2) // 3
N_CONV = (DEPTH + 1) // 3
N_POOL = DEPTH // 3
N_HEADS = 8
HEAD_DIM = 64
V_DIM = 2 * HEAD_DIM
ROPE_THETA = 10000.0
Q_BLOCK = 128
DENSE_MAX_KEYS = 2048
CONV_WIDTH = 31
POOL_WINDOWS = (2, 4, 8, 16)
POOL_GROUP = D_MODEL // 4
N_EXPERTS = 32
TOP_K = 4
D_EXPERT = D_MODEL
SWIGLU_ALPHA = 1.702
SWIGLU_LIMIT = 7.0
MOE_BLOCK = 128
RMS_EPS = 1e-6
LN_EPS = 1e-5

kernel_name = "hybrid_diffattn_conformer_pool_moe_dit_step"


def _rms(x, g):
    xf = x.astype(jnp.float32)
    y = xf * lax.rsqrt(jnp.mean(xf * xf, axis=-1, keepdims=True) + RMS_EPS)
    return (y * g.astype(jnp.float32)).astype(x.dtype)


def _layernorm(x, g, b):
    xf = x.astype(jnp.float32)
    mu = jnp.mean(xf, axis=-1, keepdims=True)
    var = jnp.mean(jnp.square(xf - mu), axis=-1, keepdims=True)
    y = (xf - mu) * lax.rsqrt(var + LN_EPS)
    return (y * g.astype(jnp.float32) + b.astype(jnp.float32)).astype(x.dtype)


def _axial_rope(rows):
    r = jnp.repeat(jnp.arange(rows), GRID_W).astype(jnp.float32)
    col = jnp.tile(jnp.arange(GRID_W), rows).astype(jnp.float32)
    half = HEAD_DIM // 2
    inv = ROPE_THETA ** (-jnp.arange(0, half, 2, dtype=jnp.float32) / half)
    ang_r = r[:, None] * inv[None, :]
    ang_c = col[:, None] * inv[None, :]
    return (jnp.cos(ang_r), jnp.sin(ang_r), jnp.cos(ang_c), jnp.sin(ang_c))


def _rot_half(x, cos, sin):
    n = x.shape[-1] // 2
    x1, x2 = x[..., :n], x[..., n:]
    return jnp.concatenate([x1 * cos - x2 * sin, x1 * sin + x2 * cos], axis=-1)


def _apply_rope(x, tabs):
    cos_r, sin_r, cos_c, sin_c = tabs
    half = HEAD_DIM // 2
    xf = x.astype(jnp.float32)
    y = jnp.concatenate([_rot_half(xf[..., :half], cos_r, sin_r),
                         _rot_half(xf[..., half:], cos_c, sin_c)], axis=-1)
    return y.astype(x.dtype)


def _diff_attend(q, k, v, lam):
    scale = 1.0 / math.sqrt(HEAD_DIM)

    def block(qb):
        s = jnp.einsum("bhcqd,bhckd->bhcqk", qb, k).astype(jnp.float32) * scale
        p = jax.nn.softmax(s, axis=-1)
        a = p[:, :, 0] - lam * p[:, :, 1]
        return jnp.einsum("bhqk,bhkd->bhqd", a.astype(v.dtype), v)

    if k.shape[3] < DENSE_MAX_KEYS:
        return block(q)
    B, H, _, Sq, dh = q.shape
    nb = Sq // Q_BLOCK
    qb = jnp.moveaxis(q.reshape(B, H, 2, nb, Q_BLOCK, dh), 3, 0)
    out = lax.map(block, qb)
    return jnp.moveaxis(out, 0, 2).reshape(B, H, Sq, V_DIM)


def _diff_attention(h, w_qkv, w_o, q_g, k_g, lq1, lk1, lq2, lk2, sub_g, lam_init, rope, ctx_k, ctx_v):
    B, S, _ = h.shape
    qk_w = N_HEADS * 2 * HEAD_DIM
    qkv = h @ w_qkv
    q, k, v = jnp.split(qkv, [qk_w, 2 * qk_w], axis=-1)
    q = _rms(q.reshape(B, S, N_HEADS, 2, HEAD_DIM), q_g).transpose(0, 2, 3, 1, 4)
    k = _rms(k.reshape(B, S, N_HEADS, 2, HEAD_DIM), k_g).transpose(0, 2, 3, 1, 4)
    v = v.reshape(B, S, N_HEADS, V_DIM).transpose(0, 2, 1, 3)
    if rope is not None:
        q = _apply_rope(q, rope)
        k = _apply_rope(k, rope)
    if ctx_k is not None:
        k_all = jnp.concatenate([k, ctx_k.astype(k.dtype)], axis=3)
        v_all = jnp.concatenate([v, ctx_v.astype(v.dtype)], axis=2)
    else:
        k_all, v_all = k, v
    lam = (jnp.exp(jnp.sum(lq1.astype(jnp.float32) * lk1.astype(jnp.float32)))
           - jnp.exp(jnp.sum(lq2.astype(jnp.float32) * lk2.astype(jnp.float32))) + lam_init)
    o = _diff_attend(q, k_all, v_all, lam)
    o = _rms(o, sub_g) * (1.0 - lam_init)
    o = o.transpose(0, 2, 1, 3).reshape(B, S, N_HEADS * V_DIM)
    return o @ w_o, k, v


def _conformer_conv(h, w_pw1, b_pw1, w_dw, b_dw, ln_g, ln_b, w_pw2, b_pw2):
    a, gt = jnp.split(h @ w_pw1 + b_pw1, 2, axis=-1)
    u = a * jax.nn.sigmoid(gt)
    pad = CONV_WIDTH // 2
    u = lax.conv_general_dilated(u, w_dw[:, None, :].astype(u.dtype), window_strides=(1,),
                                 padding=[(pad, pad)], dimension_numbers=("NWC", "WIO", "NWC"),
                                 feature_group_count=D_MODEL) + b_dw
    u = jax.nn.silu(_layernorm(u, ln_g, ln_b))
    return u @ w_pw2 + b_pw2


def _pool_mixer(h, w_pool, pool_scale):
    B, S, D = h.shape
    hf = h.astype(jnp.float32)
    cs = jnp.concatenate([jnp.zeros((B, 1, D), jnp.float32), jnp.cumsum(hf, axis=1)], axis=1)
    t = jnp.arange(S)
    means = []
    for g, w in enumerate(POOL_WINDOWS):
        lo = jnp.clip(t - w // 2, 0, S)
        hi = jnp.clip(t + (w - w // 2), 0, S)
        csg = cs[:, :, g * POOL_GROUP:(g + 1) * POOL_GROUP]
        cnt = (hi - lo).astype(jnp.float32)[None, :, None]
        means.append((csg[:, hi] - csg[:, lo]) / cnt)
    pooled = (jnp.concatenate(means, axis=-1) - hf).astype(h.dtype).reshape(B, S, 4, POOL_GROUP)
    y = jnp.einsum("bsgc,gcd->bsgd", pooled, w_pool).reshape(B, S, D)
    return y * pool_scale


def _moe(h, w_router, b_router, w_gate_up, b_gate_up, w_down, b_down):
    N, D = h.shape
    logits = (h @ w_router).astype(jnp.float32) + b_router.astype(jnp.float32)
    top_val, top_idx = lax.top_k(logits, TOP_K)
    gates = jax.nn.softmax(top_val, axis=-1).astype(h.dtype)
    NK = N * TOP_K
    flat_e = top_idx.reshape(NK)
    flat_tok = jnp.arange(NK, dtype=jnp.int32) // TOP_K
    order = jnp.argsort(flat_e)
    sorted_e = flat_e[order]
    counts = jnp.bincount(flat_e, length=N_EXPERTS)
    padded = ((counts + MOE_BLOCK - 1) // MOE_BLOCK) * MOE_BLOCK
    pad_end = jnp.cumsum(padded)
    pad_start = pad_end - padded
    start = jnp.cumsum(counts) - counts
    dest = pad_start[sorted_e] + (jnp.arange(NK) - start[sorted_e])
    n_blocks = -(-NK // MOE_BLOCK) + N_EXPERTS
    R = n_blocks * MOE_BLOCK
    slot_tok = jnp.full((R,), N, jnp.int32).at[dest].set(flat_tok[order])
    slot_w = jnp.zeros((R,), h.dtype).at[dest].set(gates.reshape(NK)[order])
    block_e = jnp.clip(jnp.searchsorted(pad_end // MOE_BLOCK, jnp.arange(n_blocks), side="right"),
                       0, N_EXPERTS - 1)
    h_pad = jnp.concatenate([h, jnp.zeros((1, D), h.dtype)], axis=0)
    xs = h_pad[slot_tok].reshape(n_blocks, MOE_BLOCK, D)

    def run(args):
        xb, e = args
        gu = xb @ w_gate_up[e] + b_gate_up[e]
        gate = jnp.minimum(gu[:, :D_EXPERT], SWIGLU_LIMIT)
        up = jnp.clip(gu[:, D_EXPERT:], -SWIGLU_LIMIT, SWIGLU_LIMIT)
        glu = gate * jax.nn.sigmoid(SWIGLU_ALPHA * gate)
        return ((up + 1.0) * glu) @ w_down[e] + b_down[e]

    ys = lax.map(run, (xs, block_e)).reshape(R, D) * slot_w[:, None]
    return jax.ops.segment_sum(ys, slot_tok, num_segments=N + 1)[:N]


def _trunk(x, cond, rope, ctx_k, ctx_v, p):
    B, S, D = x.shape
    ks, vs = [], []
    for i in range(DEPTH):
        mod = jax.nn.silu(cond) @ p["w_ada"][i] + p["b_ada"][i]
        sh1, sc1, g1, sh2, sc2, g2 = [m[:, None, :] for m in jnp.split(mod, 6, axis=-1)]
        h = _rms(x, p["norm1_g"][i]) * (1.0 + sc1) + sh1
        j = i // N_MIXERS
        kind = i % N_MIXERS
        if kind == 0:
            lam_init = 0.8 - 0.6 * math.exp(-0.3 * i)
            ck = None if ctx_k is None else ctx_k[:, j]
            cv = None if ctx_v is None else ctx_v[:, j]
            out, k, v = _diff_attention(h, p["attn_w_qkv"][j], p["attn_w_o"][j], p["attn_q_g"][j],
                                        p["attn_k_g"][j], p["attn_lq1"][j], p["attn_lk1"][j],
                                        p["attn_lq2"][j], p["attn_lk2"][j], p["attn_sub_g"][j],
                                        lam_init, rope, ck, cv)
            ks.append(k)
            vs.append(v)
        elif kind == 1:
            out = _conformer_conv(h, p["conv_w_pw1"][j], p["conv_b_pw1"][j], p["conv_w_dw"][j],
                                  p["conv_b_dw"][j], p["conv_ln_g"][j], p["conv_ln_b"][j],
                                  p["conv_w_pw2"][j], p["conv_b_pw2"][j])
        else:
            out = _pool_mixer(h, p["pool_w"][j], p["pool_scale"][j])
        x = x + g1 * out
        h = _rms(x, p["norm2_g"][i]) * (1.0 + sc2) + sh2
        f = _moe(h.reshape(B * S, D), p["moe_w_router"][i], p["moe_b_router"][i], p["moe_w_gate_up"][i],
                 p["moe_b_gate_up"][i], p["moe_w_down"][i], p["moe_b_down"][i])
        x = x + g2 * f.reshape(B, S, D)
    return x, ks, vs


def setup_inputs(seed: int = 0) -> dict:
    key = jax.random.key(seed)
    keys = iter(jax.random.split(key, 64))
    D = D_MODEL
    E = N_EXPERTS
    F = D_EXPERT

    def nrm(shape, scale):
        return jax.random.normal(next(keys), shape, jnp.float32) * scale

    def gain(shape):
        return 1.0 + nrm(shape, 0.02)

    return {
        "x_prompt": nrm((BATCH, SEQ, D), 1.0),
        "x_sample": nrm((DEC_BATCH, DEC_SEQ, D), 1.0),
        "cache_k": nrm((DEC_BATCH, N_ATTN, N_HEADS, 2, PAST_LEN, HEAD_DIM), 1.0),
        "cache_v": nrm((DEC_BATCH, N_ATTN, N_HEADS, PAST_LEN, V_DIM), 1.0),
        "c": nrm((DEC_BATCH, D), 1.0),
        "c_ctx": nrm((D,), 1.0),
        "norm1_g": gain((DEPTH, D)),
        "norm2_g": gain((DEPTH, D)),
        "w_ada": nrm((DEPTH, D, 6 * D), 0.5 * D ** -0.5),
        "b_ada": nrm((DEPTH, 6 * D), 0.02),
        "attn_w_qkv": nrm((N_ATTN, D, 2 * N_HEADS * 2 * HEAD_DIM + N_HEADS * V_DIM), D ** -0.5),
        "attn_w_o": nrm((N_ATTN, N_HEADS * V_DIM, D), (N_HEADS * V_DIM) ** -0.5),
        "attn_q_g": gain((N_ATTN, HEAD_DIM)),
        "attn_k_g": gain((N_ATTN, HEAD_DIM)),
        "attn_lq1": nrm((N_ATTN, HEAD_DIM), 0.1),
        "attn_lk1": nrm((N_ATTN, HEAD_DIM), 0.1),
        "attn_lq2": nrm((N_ATTN, HEAD_DIM), 0.1),
        "attn_lk2": nrm((N_ATTN, HEAD_DIM), 0.1),
        "attn_sub_g": gain((N_ATTN, V_DIM)),
        "conv_w_pw1": nrm((N_CONV, D, 2 * D), D ** -0.5),
        "conv_b_pw1": nrm((N_CONV, 2 * D), 0.02),
        "conv_w_dw": nrm((N_CONV, CONV_WIDTH, D), CONV_WIDTH ** -0.5),
        "conv_b_dw": nrm((N_CONV, D), 0.02),
        "conv_ln_g": gain((N_CONV, D)),
        "conv_ln_b": nrm((N_CONV, D), 0.02),
        "conv_w_pw2": nrm((N_CONV, D, D), D ** -0.5),
        "conv_b_pw2": nrm((N_CONV, D), 0.02),
        "pool_w": nrm((N_POOL, 4, POOL_GROUP, POOL_GROUP), POOL_GROUP ** -0.5),
        "pool_scale": gain((N_POOL, D)),
        "moe_w_router": nrm((DEPTH, D, E), D ** -0.5),
        "moe_b_router": nrm((DEPTH, E), 0.01),
        "moe_w_gate_up": nrm((DEPTH, E, D, 2 * F), D ** -0.5),
        "moe_b_gate_up": nrm((DEPTH, E, 2 * F), 0.02),
        "moe_w_down": nrm((DEPTH, E, F, D), F ** -0.5),
        "moe_b_down": nrm((DEPTH, E, D), 0.02),
    }


def reference(x_prompt, x_sample, cache_k, cache_v, c, c_ctx, norm1_g, norm2_g, w_ada, b_ada,
              attn_w_qkv, attn_w_o, attn_q_g, attn_k_g, attn_lq1, attn_lk1, attn_lq2, attn_lk2,
              attn_sub_g, conv_w_pw1, conv_b_pw1, conv_w_dw, conv_b_dw, conv_ln_g, conv_ln_b,
              conv_w_pw2, conv_b_pw2, pool_w, pool_scale, moe_w_router, moe_b_router,
              moe_w_gate_up, moe_b_gate_up, moe_w_down, moe_b_down):
    p = {
        "norm1_g": norm1_g, "norm2_g": norm2_g, "w_ada": w_ada, "b_ada": b_ada,
        "attn_w_qkv": attn_w_qkv, "attn_w_o": attn_w_o, "attn_q_g": attn_q_g, "attn_k_g": attn_k_g,
        "attn_lq1": attn_lq1, "attn_lk1": attn_lk1, "attn_lq2": attn_lq2, "attn_lk2": attn_lk2,
        "attn_sub_g": attn_sub_g,
        "conv_w_pw1": conv_w_pw1, "conv_b_pw1": conv_b_pw1, "conv_w_dw": conv_w_dw,
        "conv_b_dw": conv_b_dw, "conv_ln_g": conv_ln_g, "conv_ln_b": conv_ln_b,
        "conv_w_pw2": conv_w_pw2, "conv_b_pw2": conv_b_pw2,
        "pool_w": pool_w, "pool_scale": pool_scale,
        "moe_w_router": moe_w_router, "moe_b_router": moe_b_router,
        "moe_w_gate_up": moe_w_gate_up, "moe_b_gate_up": moe_b_gate_up,
        "moe_w_down": moe_w_down, "moe_b_down": moe_b_down,
    }
    y_prompt, ks, vs = _trunk(x_prompt, c_ctx[None, :], None, None, None, p)
    new_cache_k = jnp.stack(ks, axis=1)
    new_cache_v = jnp.stack(vs, axis=1)
    rows = x_sample.shape[1] // GRID_W
    rope = _axial_rope(rows)
    y_sample, _, _ = _trunk(x_sample, c, rope, cache_k, cache_v, p)
    return (y_prompt, y_sample, new_cache_k, new_cache_v)
```

```python
import functools
import math

import jax
import jax.numpy as jnp
from jax import lax
from jax.experimental import pallas as pl
from jax.experimental.pallas import tpu as pltpu

F32 = jnp.float32
BF16 = jnp.bfloat16

D = 1024
DEPTH = 4
BATCH, SEQ = 32, 256
DEC_BATCH, DEC_SEQ = 4, 1024
PAST_LEN = 256
GRID_W = 64
N_MIXERS = 3
H = 8
HEAD_DIM = 64
V_DIM = 2 * HEAD_DIM
QK_W = H * 2 * HEAD_DIM
ROPE_THETA = 10000.0
CONV_WIDTH = 31
POOL_WINDOWS = (2, 4, 8, 16)
POOL_GROUP = D // 4
E = 32
TOP_K = 4
FF = D
SWIGLU_ALPHA = 1.702
SWIGLU_LIMIT = 7.0
RMS_EPS = 1e-6
LN_EPS = 1e-5

N_P = BATCH * SEQ
N_S = DEC_BATCH * DEC_SEQ
N_TOK = N_P + N_S
N_COND = 8

TM = 256
NK = N_TOK * TOP_K
N_TILES = NK // TM + E
R_SLOTS = N_TILES * TM
HALO = 16
ROW_BLK = 32
GATHER_CHUNK = 256
VMEM_LIMIT = 56 * 1024 * 1024


def _cparams(sem=None):
    return pltpu.CompilerParams(dimension_semantics=sem, vmem_limit_bytes=VMEM_LIMIT)


def _cond_row(tile, rows_per_tile):
    n_prompt_tiles = N_P // rows_per_tile
    tiles_per_seq = DEC_SEQ // rows_per_tile
    return jnp.where(tile < n_prompt_tiles, 0, 1 + (tile - n_prompt_tiles) // tiles_per_seq)


def _mod_spec(layer, row_of_step):
    return pl.BlockSpec((None, None, 1, 6 * D), lambda b, *_: (layer, row_of_step(b), 0, 0))


def _full(shape):
    nd = len(shape)
    return pl.BlockSpec(shape, lambda *_: (0,) * nd)


def _rms_mod(x, g, sc, sh):
    ms = jnp.mean(x * x, axis=-1, keepdims=True)
    return (x * lax.rsqrt(ms + RMS_EPS) * g) * (1.0 + sc) + sh


def _ada_body(cond_ref, w_ref, b_ref, o_ref):
    c = cond_ref[...]
    a = (c * jax.nn.sigmoid(c)).astype(BF16)
    o_ref[0] = jnp.dot(a, w_ref[0].astype(BF16), preferred_element_type=F32) + b_ref[0]


def _ada_table(cond, w_ada, b_ada):
    tn = 1536
    return pl.pallas_call(
        _ada_body,
        grid=(DEPTH, 6 * D // tn),
        in_specs=[
            pl.BlockSpec((N_COND, D), lambda i, j: (0, 0)),
            pl.BlockSpec((1, D, tn), lambda i, j: (i, 0, j)),
            pl.BlockSpec((1, 1, tn), lambda i, j: (i, 0, j)),
        ],
        out_specs=pl.BlockSpec((1, N_COND, tn), lambda i, j: (i, 0, j)),
        out_shape=jax.ShapeDtypeStruct((DEPTH, N_COND, 6 * D), F32),
        compiler_params=_cparams(("arbitrary", "arbitrary")),
        name="ada_table",
    )(cond, w_ada, b_ada.reshape(DEPTH, 1, 6 * D))


def _half_norm(x, g, lo):
    ss = x * x
    s_lo = jnp.sum(jnp.where(lo, ss, 0.0), axis=-1, keepdims=True)
    s_hi = jnp.sum(jnp.where(lo, 0.0, ss), axis=-1, keepdims=True)
    ms = jnp.where(lo, s_lo, s_hi) * (1.0 / HEAD_DIM)
    return x * lax.rsqrt(ms + RMS_EPS) * g


def _rope(x, cos, sin_signed, first16):
    partner = jnp.where(first16, pltpu.roll(x, 128 - 16, axis=1), pltpu.roll(x, 16, axis=1))
    return x * cos + partner * sin_signed


def _dot_nt(a, b):
    return lax.dot_general(a, b, (((1,), (1,)), ((), ())), preferred_element_type=F32)


def _attn_body(*refs, seq, lam_init, latent, q_blk):
    it = iter(refs)
    x_ref, mod_ref, n1g_ref, wqkv_ref, wo_ref, qg_ref, kg_ref = (next(it) for _ in range(7))
    lq1_ref, lk1_ref, lq2_ref, lk2_ref, subg_ref = (next(it) for _ in range(5))
    if latent:
        cos_ref, sin_ref, ck_ref, cv_ref = (next(it) for _ in range(4))
    xo_ref = next(it)
    if not latent:
        ko_ref, vo_ref = next(it), next(it)
    qkv_s, qlo_s, qhi_s, kb_s, vb_s, oh_s, o_s = (next(it) for _ in range(7))

    mod = mod_ref[...]
    sh1, sc1, g1 = mod[:, 0:D], mod[:, D:2 * D], mod[:, 2 * D:3 * D]
    n_col = (2 * QK_W + H * V_DIM) // 128
    for c in range(seq // TM):
        rs = slice(c * TM, (c + 1) * TM)
        h = _rms_mod(x_ref[rs, :], n1g_ref[...], sc1, sh1).astype(BF16)
        qkv = jnp.dot(h, wqkv_ref[...], preferred_element_type=F32)
        for cb in range(n_col):
            qkv_s[cb, rs, :] = qkv[:, cb * 128:(cb + 1) * 128]

    lam = (jnp.exp(jnp.sum(lq1_ref[...] * lk1_ref[...], axis=-1, keepdims=True))
           - jnp.exp(jnp.sum(lq2_ref[...] * lk2_ref[...], axis=-1, keepdims=True)) + lam_init)

    lane = lax.broadcasted_iota(jnp.int32, (1, 2 * HEAD_DIM), 1)
    lo = lane < HEAD_DIM
    first16 = (lane % 32) < 16
    scale = 1.0 / math.sqrt(HEAD_DIM)

    def head(hd, carry):
        qn = _half_norm(qkv_s[hd], qg_ref[...], lo)
        kn = _half_norm(qkv_s[H + hd], kg_ref[...], lo)
        vh = qkv_s[2 * H + hd]
        if latent:
            qn = _rope(qn, cos_ref[...], sin_ref[...], first16)
            kn = _rope(kn, cos_ref[...], sin_ref[...], first16)
        else:
            ko_ref[hd, 0] = kn[:, :HEAD_DIM]
            ko_ref[hd, 1] = kn[:, HEAD_DIM:]
            vo_ref[hd] = vh
        qs = qn * scale
        qlo_s[...] = jnp.where(lo, qs, 0.0).astype(BF16)
        qhi_s[...] = jnp.where(lo, 0.0, qs).astype(BF16)
        kb_s[0:seq, :] = kn.astype(BF16)
        vb_s[0:seq, :] = vh.astype(BF16)
        if latent:
            kb_s[seq:seq + PAST_LEN, :] = ck_ref[hd].astype(BF16)
            vb_s[seq:seq + PAST_LEN, :] = cv_ref[hd].astype(BF16)

        def q_block(qb, carry2):
            r0 = pl.multiple_of(qb * q_blk, q_blk)
            es, dens = [], []
            for q_s in (qlo_s, qhi_s):
                s = _dot_nt(q_s[pl.ds(r0, q_blk), :], kb_s[...])
                e = jnp.exp(s - jnp.max(s, axis=-1, keepdims=True))
                es.append(e)
                dens.append(jnp.sum(e, axis=-1, keepdims=True))
            a = (es[0] * (1.0 / dens[0]) - es[1] * (lam / dens[1])).astype(BF16)
            o = jnp.dot(a, vb_s[...], preferred_element_type=F32)
            on = o * lax.rsqrt(jnp.mean(o * o, axis=-1, keepdims=True) + RMS_EPS) * subg_ref[...] * (1.0 - lam_init)
            oh_s[hd, pl.ds(r0, q_blk), :] = on.astype(BF16)
            return carry2

        lax.fori_loop(0, seq // q_blk, q_block, 0)
        return carry

    lax.fori_loop(0, H, head, 0)

    for hd in range(H):
        o_s[:, hd * 128:(hd + 1) * 128] = oh_s[hd]
    for c in range(seq // TM):
        rs = slice(c * TM, (c + 1) * TM)
        out = jnp.dot(o_s[rs, :], wo_ref[...], preferred_element_type=F32)
        xo_ref[rs, :] = x_ref[rs, :] + g1 * out


def _attention(x_all, mod, layer, j, latent, p, rope_tabs, ck, cv):
    seq = DEC_SEQ if latent else SEQ
    nb = DEC_BATCH if latent else BATCH
    blk_off = N_P // seq if latent else 0
    n_keys = seq + PAST_LEN if latent else seq
    lam_init = 0.8 - 0.6 * math.exp(-0.3 * layer)
    row = (lambda b: 1 + b) if latent else (lambda b: 0)
    two = lambda v: jnp.concatenate([v, v])[None, :]
    vec = lambda v: v[None, :]

    args = [x_all, mod, vec(p["norm1_g"][layer]), p["attn_w_qkv"][j].astype(BF16), p["attn_w_o"][j].astype(BF16),
            two(p["attn_q_g"][j]), two(p["attn_k_g"][j]), vec(p["attn_lq1"][j]), vec(p["attn_lk1"][j]),
            vec(p["attn_lq2"][j]), vec(p["attn_lk2"][j]), vec(p["attn_sub_g"][j])]
    in_specs = [pl.BlockSpec((seq, D), lambda b: (blk_off + b, 0)), _mod_spec(layer, row), _full((1, D)),
                _full((D, 2 * QK_W + H * V_DIM)), _full((H * V_DIM, D)), _full((1, 128)), _full((1, 128)),
                _full((1, HEAD_DIM)), _full((1, HEAD_DIM)), _full((1, HEAD_DIM)), _full((1, HEAD_DIM)),
                _full((1, V_DIM))]
    out_shape = [jax.ShapeDtypeStruct((N_TOK, D), F32)]
    out_specs = [pl.BlockSpec((seq, D), lambda b: (blk_off + b, 0))]
    if latent:
        args += [rope_tabs[0], rope_tabs[1], ck, cv]
        in_specs += [_full((seq, 128)), _full((seq, 128)),
                     pl.BlockSpec((None, None, H, PAST_LEN, 128), lambda b: (b, j, 0, 0, 0)),
                     pl.BlockSpec((None, None, H, PAST_LEN, V_DIM), lambda b: (b, j, 0, 0, 0))]
    else:
        out_shape += [jax.ShapeDtypeStruct((nb, H, 2, seq, HEAD_DIM), F32),
                      jax.ShapeDtypeStruct((nb, H, seq, V_DIM), F32)]
        out_specs += [pl.BlockSpec((None, H, 2, seq, HEAD_DIM), lambda b: (b, 0, 0, 0, 0)),
                      pl.BlockSpec((None, H, seq, V_DIM), lambda b: (b, 0, 0, 0))]
    return pl.pallas_call(
        functools.partial(_attn_body, seq=seq, lam_init=lam_init, latent=latent, q_blk=256),
        grid=(nb,),
        in_specs=in_specs,
        out_specs=out_specs,
        out_shape=out_shape,
        scratch_shapes=[pltpu.VMEM(((2 * QK_W + H * V_DIM) // 128, seq, 128), F32),
                        pltpu.VMEM((seq, 128), BF16), pltpu.VMEM((seq, 128), BF16),
                        pltpu.VMEM((n_keys, 128), BF16), pltpu.VMEM((n_keys, V_DIM), BF16),
                        pltpu.VMEM((H, seq, V_DIM), BF16), pltpu.VMEM((seq, H * V_DIM), BF16)],
        input_output_aliases={0: 0},
        compiler_params=_cparams(("arbitrary",)),
        name="attn_latent" if latent else "attn_prompt",
    )(*args)


def _conv_body(x_ref, mod_ref, n1g_ref, w1_ref, b1_ref, wdw_ref, bdw_ref, lng_ref, lnb_ref, w2_ref, b2_ref,
               xo_ref, upad, cv_s, *, seq):
    mod = mod_ref[...]
    sh1, sc1, g1 = mod[:, 0:D], mod[:, D:2 * D], mod[:, 2 * D:3 * D]
    zeros = jnp.zeros((HALO, D), F32)
    upad[0:HALO, :] = zeros
    upad[HALO + seq:2 * HALO + seq, :] = zeros
    for c in range(seq // TM):
        rs = slice(c * TM, (c + 1) * TM)
        h = _rms_mod(x_ref[rs, :], n1g_ref[...], sc1, sh1).astype(BF16)
        y = jnp.dot(h, w1_ref[...], preferred_element_type=F32) + b1_ref[...]
        upad[HALO + c * TM:HALO + (c + 1) * TM, :] = y[:, :D] * jax.nn.sigmoid(y[:, D:])

    pad = CONV_WIDTH // 2
    cb = 256

    def conv_rows(rb, carry):
        r0 = pl.multiple_of(rb * ROW_BLK, ROW_BLK)
        for c in range(D // cb):
            cs = slice(c * cb, (c + 1) * cb)
            win = upad[pl.ds(r0, ROW_BLK + 2 * HALO), cs]
            acc = jnp.zeros((ROW_BLK, cb), F32) + bdw_ref[:, cs]
            for k in range(CONV_WIDTH):
                off = HALO - pad + k
                acc = acc + win[off:off + ROW_BLK, :] * wdw_ref[k:k + 1, cs]
            cv_s[pl.ds(r0, ROW_BLK), cs] = acc
        return carry

    lax.fori_loop(0, seq // ROW_BLK, conv_rows, 0)

    for c in range(seq // TM):
        rs = slice(c * TM, (c + 1) * TM)
        v = cv_s[rs, :]
        mu = jnp.mean(v, axis=-1, keepdims=True)
        var = jnp.mean(jnp.square(v - mu), axis=-1, keepdims=True)
        yn = (v - mu) * lax.rsqrt(var + LN_EPS) * lng_ref[...] + lnb_ref[...]
        u2 = (yn * jax.nn.sigmoid(yn)).astype(BF16)
        out = jnp.dot(u2, w2_ref[...], preferred_element_type=F32) + b2_ref[...]
        xo_ref[rs, :] = x_ref[rs, :] + g1 * out


def _conformer(x_all, mod, layer, j, latent, p):
    seq = DEC_SEQ if latent else SEQ
    nb = DEC_BATCH if latent else BATCH
    blk_off = N_P // seq if latent else 0
    row = (lambda b: 1 + b) if latent else (lambda b: 0)
    vec = lambda v: v[None, :]
    return pl.pallas_call(
        functools.partial(_conv_body, seq=seq),
        grid=(nb,),
        in_specs=[pl.BlockSpec((seq, D), lambda b: (blk_off + b, 0)), _mod_spec(layer, row), _full((1, D)),
                  _full((D, 2 * D)), _full((1, 2 * D)), _full((CONV_WIDTH, D)), _full((1, D)), _full((1, D)),
                  _full((1, D)), _full((D, D)), _full((1, D))],
        out_specs=pl.BlockSpec((seq, D), lambda b: (blk_off + b, 0)),
        out_shape=jax.ShapeDtypeStruct((N_TOK, D), F32),
        scratch_shapes=[pltpu.VMEM((seq + 2 * HALO, D), F32), pltpu.VMEM((seq, D), F32)],
        input_output_aliases={0: 0},
        compiler_params=_cparams(("arbitrary",)),
        name="conv_latent" if latent else "conv_prompt",
    )(x_all, mod, vec(p["norm1_g"][layer]), p["conv_w_pw1"][j].astype(BF16), vec(p["conv_b_pw1"][j]),
      p["conv_w_dw"][j], vec(p["conv_b_dw"][j]), vec(p["conv_ln_g"][j]), vec(p["conv_ln_b"][j]),
      p["conv_w_pw2"][j].astype(BF16), vec(p["conv_b_pw2"][j]))


def _pool_body(x_ref, mod_ref, n1g_ref, wp_ref, ps_ref, xo_ref, hpad, p_s, *, seq):
    mod = mod_ref[...]
    sh1, sc1, g1 = mod[:, 0:D], mod[:, D:2 * D], mod[:, 2 * D:3 * D]
    zeros = jnp.zeros((HALO, D), F32)
    hpad[0:HALO, :] = zeros
    hpad[HALO + seq:2 * HALO + seq, :] = zeros
    for c in range(seq // TM):
        rs = slice(c * TM, (c + 1) * TM)
        hpad[HALO + c * TM:HALO + (c + 1) * TM, :] = _rms_mod(x_ref[rs, :], n1g_ref[...], sc1, sh1)

    def pool_rows(rb, carry):
        r0 = pl.multiple_of(rb * ROW_BLK, ROW_BLK)
        t = r0 + lax.broadcasted_iota(jnp.int32, (ROW_BLK, 1), 0)
        for g, w in enumerate(POOL_WINDOWS):
            cs = slice(g * POOL_GROUP, (g + 1) * POOL_GROUP)
            win = hpad[pl.ds(r0, ROW_BLK + 2 * HALO), cs]
            acc = jnp.zeros((ROW_BLK, POOL_GROUP), F32)
            for d in range(-(w // 2), w - w // 2):
                acc = acc + win[HALO + d:HALO + d + ROW_BLK, :]
            cnt = jnp.minimum(t + (w - w // 2), seq) - jnp.maximum(t - w // 2, 0)
            pooled = acc / cnt.astype(F32) - win[HALO:HALO + ROW_BLK, :]
            p_s[pl.ds(r0, ROW_BLK), cs] = pooled.astype(BF16)
        return carry

    lax.fori_loop(0, seq // ROW_BLK, pool_rows, 0)

    for c in range(seq // TM):
        rs = slice(c * TM, (c + 1) * TM)
        for g in range(len(POOL_WINDOWS)):
            cs = slice(g * POOL_GROUP, (g + 1) * POOL_GROUP)
            y = jnp.dot(p_s[rs, cs], wp_ref[g], preferred_element_type=F32) * ps_ref[:, cs]
            xo_ref[rs, cs] = x_ref[rs, cs] + g1[:, cs] * y


def _pool_mixer(x_all, mod, layer, j, latent, p):
    seq = DEC_SEQ if latent else SEQ
    nb = DEC_BATCH if latent else BATCH
    blk_off = N_P // seq if latent else 0
    row = (lambda b: 1 + b) if latent else (lambda b: 0)
    ng = len(POOL_WINDOWS)
    return pl.pallas_call(
        functools.partial(_pool_body, seq=seq),
        grid=(nb,),
        in_specs=[pl.BlockSpec((seq, D), lambda b: (blk_off + b, 0)), _mod_spec(layer, row), _full((1, D)),
                  _full((ng, POOL_GROUP, POOL_GROUP)), _full((1, D))],
        out_specs=pl.BlockSpec((seq, D), lambda b: (blk_off + b, 0)),
        out_shape=jax.ShapeDtypeStruct((N_TOK, D), F32),
        scratch_shapes=[pltpu.VMEM((seq + 2 * HALO, D), F32), pltpu.VMEM((seq, D), BF16)],
        input_output_aliases={0: 0},
        compiler_params=_cparams(("arbitrary",)),
        name="pool_latent" if latent else "pool_prompt",
    )(x_all, mod, p["norm1_g"][layer][None, :], p["pool_w"][j].astype(BF16), p["pool_scale"][j][None, :])


def _router_body(x_ref, mod_ref, n2g_ref, wr_ref, br_ref, h_ref, idx_ref, gate_ref):
    mod = mod_ref[...]
    sh2, sc2 = mod[:, 3 * D:4 * D], mod[:, 4 * D:5 * D]
    h = _rms_mod(x_ref[...], n2g_ref[...], sc2, sh2)
    h_ref[...] = h
    logits = jnp.dot(h.astype(BF16), wr_ref[...], preferred_element_type=F32) + br_ref[...]
    lane = lax.broadcasted_iota(jnp.int32, logits.shape, 1)
    neg_inf = float("-inf")
    vals, idxs = [], []
    for _ in range(TOP_K):
        m = jnp.max(logits, axis=-1, keepdims=True)
        ix = jnp.min(jnp.where(logits == m, lane, 128), axis=-1, keepdims=True)
        vals.append(m)
        idxs.append(ix)
        logits = jnp.where(lane == ix, neg_inf, logits)
    es = [jnp.exp(v - vals[0]) for v in vals]
    den = es[0] + es[1] + es[2] + es[3]
    idx_out = jnp.zeros(lane.shape, jnp.int32)
    gate_out = jnp.zeros(lane.shape, F32)
    for k in range(TOP_K):
        idx_out = jnp.where(lane == k, idxs[k], idx_out)
        gate_out = jnp.where(lane == k, es[k] / den, gate_out)
    idx_ref[...] = idx_out
    gate_ref[...] = gate_out


def _router(x_all, mod, layer, p):
    wr = jnp.zeros((D, 128), BF16).at[:, :E].set(p["moe_w_router"][layer].astype(BF16))
    br = jnp.full((1, 128), float("-inf"), F32).at[0, :E].set(p["moe_b_router"][layer])
    row = lambda t: _cond_row(t, TM)
    return pl.pallas_call(
        _router_body,
        grid=(N_TOK // TM,),
        in_specs=[pl.BlockSpec((TM, D), lambda t: (t, 0)), _mod_spec(layer, row), _full((1, D)),
                  _full((D, 128)), _full((1, 128))],
        out_specs=[pl.BlockSpec((TM, D), lambda t: (t, 0)), pl.BlockSpec((TM, 128), lambda t: (t, 0)),
                   pl.BlockSpec((TM, 128), lambda t: (t, 0))],
        out_shape=[jax.ShapeDtypeStruct((N_TOK, D), F32), jax.ShapeDtypeStruct((N_TOK, 128), jnp.int32),
                   jax.ShapeDtypeStruct((N_TOK, 128), F32)],
        compiler_params=_cparams(("arbitrary",)),
        name="router",
    )(x_all, mod, p["norm2_g"][layer][None, :], wr, br)


def _dispatch_plan(top_idx):
    flat_e = top_idx.reshape(NK)
    onehot = (flat_e[:, None] == jnp.arange(E, dtype=jnp.int32)[None, :]).astype(jnp.int32)
    csum = jnp.cumsum(onehot, axis=0)
    rank = jnp.take_along_axis(csum, flat_e[:, None], axis=1)[:, 0] - 1
    counts = csum[-1]
    padded = ((counts + TM - 1) // TM) * TM
    pad_end = jnp.cumsum(padded)
    pad_start = pad_end - padded
    dest = pad_start[flat_e] + rank
    flat_tok = jnp.arange(NK, dtype=jnp.int32) // TOP_K
    slot_tok = jnp.zeros((R_SLOTS,), jnp.int32).at[dest].set(flat_tok, unique_indices=True)
    tile_expert = jnp.clip(jnp.searchsorted(pad_end // TM, jnp.arange(N_TILES, dtype=jnp.int32), side="right"),
                           0, E - 1).astype(jnp.int32)
    pos_kmajor = dest.reshape(N_TOK, TOP_K).T.reshape(NK)
    return slot_tok, tile_expert, pos_kmajor.astype(jnp.int32)


def _gather_body(idx_ref, src, dst, sem, *, n_rows):
    n_chunks = n_rows // GATHER_CHUNK

    def chunk_copy(slot):
        return pltpu.make_async_copy(src.at[pl.ds(0, GATHER_CHUNK)], dst.at[pl.ds(0, GATHER_CHUNK)], sem.at[slot])

    def chunk(c, carry):
        base = c * GATHER_CHUNK
        slot = c % 2

        def row(r, carry2):
            tok = idx_ref[base + r]
            pltpu.make_async_copy(src.at[pl.ds(tok, 1)], dst.at[pl.ds(base + r, 1)], sem.at[slot]).start()
            return carry2

        lax.fori_loop(0, GATHER_CHUNK, row, 0, unroll=8)

        @pl.when(c > 0)
        def _():
            chunk_copy(1 - slot).wait()

        return carry

    lax.fori_loop(0, n_chunks, chunk, 0)
    chunk_copy((n_chunks - 1) % 2).wait()


def _gather_rows(idx, src):
    n_rows = idx.shape[0]
    return pl.pallas_call(
        functools.partial(_gather_body, n_rows=n_rows),
        grid_spec=pltpu.PrefetchScalarGridSpec(
            num_scalar_prefetch=1,
            grid=(1,),
            in_specs=[pl.BlockSpec(memory_space=pl.ANY)],
            out_specs=pl.BlockSpec(memory_space=pl.ANY),
            scratch_shapes=[pltpu.SemaphoreType.DMA((2,))],
        ),
        out_shape=jax.ShapeDtypeStruct((n_rows, D), src.dtype),
        compiler_params=_cparams(("arbitrary",)),
        name="gather_rows",
    )(idx, src)


def _expert_body(te_ref, xs_ref, wgu_ref, bgu_ref, wdn_ref, bdn_ref, ys_ref, wgu_bf, wdn_bf):
    i = pl.program_id(0)
    changed = jnp.logical_or(i == 0, te_ref[i] != te_ref[jnp.maximum(i - 1, 0)])

    @pl.when(changed)
    def _():
        for c in range(D // TM):
            rs = slice(c * TM, (c + 1) * TM)
            wgu_bf[rs, :] = wgu_ref[0, rs, :].astype(BF16)
            wdn_bf[rs, :] = wdn_ref[0, rs, :].astype(BF16)

    x = xs_ref[...].astype(BF16)
    gu = jnp.dot(x, wgu_bf[...], preferred_element_type=F32) + bgu_ref[0]
    gate = jnp.minimum(gu[:, :FF], SWIGLU_LIMIT)
    up = jnp.clip(gu[:, FF:], -SWIGLU_LIMIT, SWIGLU_LIMIT)
    glu = gate * jax.nn.sigmoid(SWIGLU_ALPHA * gate)
    act = ((up + 1.0) * glu).astype(BF16)
    ys_ref[...] = jnp.dot(act, wdn_bf[...], preferred_element_type=F32) + bdn_ref[0]


def _experts(tile_expert, xs, layer, p):
    return pl.pallas_call(
        _expert_body,
        grid_spec=pltpu.PrefetchScalarGridSpec(
            num_scalar_prefetch=1,
            grid=(N_TILES,),
            in_specs=[pl.BlockSpec((TM, D), lambda i, te: (i, 0)),
                      pl.BlockSpec((1, D, 2 * FF), lambda i, te: (te[i], 0, 0)),
                      pl.BlockSpec((1, 1, 2 * FF), lambda i, te: (te[i], 0, 0)),
                      pl.BlockSpec((1, FF, D), lambda i, te: (te[i], 0, 0)),
                      pl.BlockSpec((1, 1, D), lambda i, te: (te[i], 0, 0))],
            out_specs=pl.BlockSpec((TM, D), lambda i, te: (i, 0)),
            scratch_shapes=[pltpu.VMEM((D, 2 * FF), BF16), pltpu.VMEM((FF, D), BF16)],
        ),
        out_shape=jax.ShapeDtypeStruct((R_SLOTS, D), F32),
        compiler_params=_cparams(("arbitrary",)),
        name="experts",
    )(tile_expert, xs, p["moe_w_gate_up"][layer], p["moe_b_gate_up"][layer].reshape(E, 1, 2 * FF),
      p["moe_w_down"][layer], p["moe_b_down"][layer].reshape(E, 1, D))


def _combine_body(x_ref, mod_ref, yg_ref, gate_ref, xo_ref):
    g2 = mod_ref[:, 5 * D:6 * D]
    gates = gate_ref[...]
    f = gates[:, 0:1] * yg_ref[0]
    for k in range(1, TOP_K):
        f = f + gates[:, k:k + 1] * yg_ref[k]
    xo_ref[...] = x_ref[...] + g2 * f


def _combine(x_all, mod, layer, yg, gates):
    row = lambda t: _cond_row(t, TM)
    return pl.pallas_call(
        _combine_body,
        grid=(N_TOK // TM,),
        in_specs=[pl.BlockSpec((TM, D), lambda t: (t, 0)), _mod_spec(layer, row),
                  pl.BlockSpec((TOP_K, TM, D), lambda t: (0, t, 0)), pl.BlockSpec((TM, 128), lambda t: (t, 0))],
        out_specs=pl.BlockSpec((TM, D), lambda t: (t, 0)),
        out_shape=jax.ShapeDtypeStruct((N_TOK, D), F32),
        input_output_aliases={0: 0},
        compiler_params=_cparams(("arbitrary",)),
        name="combine",
    )(x_all, mod, yg, gates)


def _moe(x_all, mod, layer, p):
    h2, idx_pad, gates = _router(x_all, mod, layer, p)
    slot_tok, tile_expert, pos = _dispatch_plan(idx_pad[:, :TOP_K])
    xs = _gather_rows(slot_tok, h2)
    ys = _experts(tile_expert, xs, layer, p)
    yg = _gather_rows(pos, ys).reshape(TOP_K, N_TOK, D)
    return _combine(x_all, mod, layer, yg, gates)


def _rope_tables():
    rows = DEC_SEQ // GRID_W
    r = jnp.repeat(jnp.arange(rows), GRID_W).astype(F32)
    col = jnp.tile(jnp.arange(GRID_W), rows).astype(F32)
    half = HEAD_DIM // 2
    inv = ROPE_THETA ** (-jnp.arange(0, half, 2, dtype=F32) / half)
    ang_r = r[:, None] * inv[None, :]
    ang_c = col[:, None] * inv[None, :]
    cos64 = jnp.concatenate([jnp.cos(ang_r), jnp.cos(ang_r), jnp.cos(ang_c), jnp.cos(ang_c)], axis=-1)
    sin64 = jnp.concatenate([-jnp.sin(ang_r), jnp.sin(ang_r), -jnp.sin(ang_c), jnp.sin(ang_c)], axis=-1)
    return jnp.concatenate([cos64, cos64], axis=-1), jnp.concatenate([sin64, sin64], axis=-1)


def kernel(x_prompt, x_sample, cache_k, cache_v, c, c_ctx, norm1_g, norm2_g, w_ada, b_ada, attn_w_qkv, attn_w_o, attn_q_g, attn_k_g, attn_lq1, attn_lk1, attn_lq2, attn_lk2, attn_sub_g, conv_w_pw1, conv_b_pw1, conv_w_dw, conv_b_dw, conv_ln_g, conv_ln_b, conv_w_pw2, conv_b_pw2, pool_w, pool_scale, moe_w_router, moe_b_router, moe_w_gate_up, moe_b_gate_up, moe_w_down, moe_b_down):
    p = {
        "norm1_g": norm1_g, "norm2_g": norm2_g,
        "attn_w_qkv": attn_w_qkv, "attn_w_o": attn_w_o, "attn_q_g": attn_q_g, "attn_k_g": attn_k_g,
        "attn_lq1": attn_lq1, "attn_lk1": attn_lk1, "attn_lq2": attn_lq2, "attn_lk2": attn_lk2,
        "attn_sub_g": attn_sub_g,
        "conv_w_pw1": conv_w_pw1, "conv_b_pw1": conv_b_pw1, "conv_w_dw": conv_w_dw, "conv_b_dw": conv_b_dw,
        "conv_ln_g": conv_ln_g, "conv_ln_b": conv_ln_b, "conv_w_pw2": conv_w_pw2, "conv_b_pw2": conv_b_pw2,
        "pool_w": pool_w, "pool_scale": pool_scale,
        "moe_w_router": moe_w_router, "moe_b_router": moe_b_router, "moe_w_gate_up": moe_w_gate_up,
        "moe_b_gate_up": moe_b_gate_up, "moe_w_down": moe_w_down, "moe_b_down": moe_b_down,
    }
    cond = jnp.concatenate([c_ctx[None, :], c, jnp.zeros((N_COND - 1 - DEC_BATCH, D), F32)], axis=0)
    mod = _ada_table(cond, w_ada, b_ada).reshape(DEPTH, N_COND, 1, 6 * D)
    rope_tabs = _rope_tables()
    ck = cache_k.transpose(0, 1, 2, 4, 3, 5).reshape(DEC_BATCH, -1, H, PAST_LEN, 2 * HEAD_DIM)

    x_all = jnp.concatenate([x_prompt.reshape(N_P, D), x_sample.reshape(N_S, D)], axis=0)
    ks, vs = [], []
    for i in range(DEPTH):
        j, kind = i // N_MIXERS, i % N_MIXERS
        if kind == 0:
            x_all, k_new, v_new = _attention(x_all, mod, i, j, False, p, None, None, None)
            (x_all,) = _attention(x_all, mod, i, j, True, p, rope_tabs, ck, cache_v)
            ks.append(k_new)
            vs.append(v_new)
        elif kind == 1:
            x_all = _conformer(x_all, mod, i, j, False, p)
            x_all = _conformer(x_all, mod, i, j, True, p)
        else:
            x_all = _pool_mixer(x_all, mod, i, j, False, p)
            x_all = _pool_mixer(x_all, mod, i, j, True, p)
        x_all = _moe(x_all, mod, i, p)
    y_prompt = x_all[:N_P].reshape(BATCH, SEQ, D)
    y_sample = x_all[N_P:].reshape(DEC_BATCH, DEC_SEQ, D)
    return (y_prompt, y_sample, jnp.stack(ks, axis=1), jnp.stack(vs, axis=1))
```

```python
import functools
import math

import jax
import jax.numpy as jnp
from jax import lax
from jax.experimental import pallas as pl
from jax.experimental.pallas import tpu as pltpu

F32 = jnp.float32
BF16 = jnp.bfloat16

D = 1024
DEPTH = 4
BATCH, SEQ = 32, 256
DEC_BATCH, DEC_SEQ = 4, 1024
PAST_LEN = 256
GRID_W = 64
N_MIXERS = 3
H = 8
HEAD_DIM = 64
V_DIM = 2 * HEAD_DIM
QK_W = H * 2 * HEAD_DIM
ROPE_THETA = 10000.0
CONV_WIDTH = 31
POOL_WINDOWS = (2, 4, 8, 16)
POOL_GROUP = D // 4
E = 32
TOP_K = 4
FF = D
SWIGLU_ALPHA = 1.702
SWIGLU_LIMIT = 7.0
RMS_EPS = 1e-6
LN_EPS = 1e-5

N_P = BATCH * SEQ
N_S = DEC_BATCH * DEC_SEQ
N_TOK = N_P + N_S
N_COND = 8

TM = 256
NK = N_TOK * TOP_K
N_TILES = NK // TM + E
R_SLOTS = N_TILES * TM
HALO = 16
ROW_BLK = 32
GATHER_CHUNK = 256
assert TOP_K == 4
VMEM_LIMIT = 56 * 1024 * 1024


def _cparams(sem=None):
    return pltpu.CompilerParams(dimension_semantics=sem, vmem_limit_bytes=VMEM_LIMIT)


def _cond_row(tile, rows_per_tile):
    n_prompt_tiles = N_P // rows_per_tile
    tiles_per_seq = DEC_SEQ // rows_per_tile
    return jnp.where(tile < n_prompt_tiles, 0, 1 + (tile - n_prompt_tiles) // tiles_per_seq)


def _mod_spec(layer, row_of_step):
    return pl.BlockSpec((None, None, 1, 6 * D), lambda b, *_: (layer, row_of_step(b), 0, 0))


def _full(shape):
    nd = len(shape)
    return pl.BlockSpec(shape, lambda *_: (0,) * nd)


def _rms_mod(x, g, sc, sh):
    ms = jnp.mean(x * x, axis=-1, keepdims=True)
    return (x * lax.rsqrt(ms + RMS_EPS) * g) * (1.0 + sc) + sh


def _ada_body(cond_ref, w_ref, b_ref, o_ref):
    c = cond_ref[...]
    a = (c * jax.nn.sigmoid(c)).astype(BF16)
    o_ref[0] = jnp.dot(a, w_ref[0].astype(BF16), preferred_element_type=F32) + b_ref[0]


def _ada_table(cond, w_ada, b_ada):
    tn = 1536
    return pl.pallas_call(
        _ada_body,
        grid=(DEPTH, 6 * D // tn),
        in_specs=[
            pl.BlockSpec((N_COND, D), lambda i, j: (0, 0)),
            pl.BlockSpec((1, D, tn), lambda i, j: (i, 0, j)),
            pl.BlockSpec((1, 1, tn), lambda i, j: (i, 0, j)),
        ],
        out_specs=pl.BlockSpec((1, N_COND, tn), lambda i, j: (i, 0, j)),
        out_shape=jax.ShapeDtypeStruct((DEPTH, N_COND, 6 * D), F32),
        compiler_params=_cparams(("arbitrary", "arbitrary")),
        name="ada_table",
    )(cond, w_ada, b_ada.reshape(DEPTH, 1, 6 * D))


def _half_norm(x, g, lo):
    ss = x * x
    s_lo = jnp.sum(jnp.where(lo, ss, 0.0), axis=-1, keepdims=True)
    s_hi = jnp.sum(jnp.where(lo, 0.0, ss), axis=-1, keepdims=True)
    ms = jnp.where(lo, s_lo, s_hi) * (1.0 / HEAD_DIM)
    return x * lax.rsqrt(ms + RMS_EPS) * g


def _rope(x, cos, sin_signed, first16):
    partner = jnp.where(first16, pltpu.roll(x, 128 - 16, axis=1), pltpu.roll(x, 16, axis=1))
    return x * cos + partner * sin_signed


def _dot_nt(a, b):
    return lax.dot_general(a, b, (((1,), (1,)), ((), ())), preferred_element_type=F32)


def _attn_body(*refs, seq, lam_init, latent, q_blk, kv_alias):
    it = iter(refs)
    x_ref, mod_ref, n1g_ref, wqkv_ref, wo_ref, qg_ref, kg_ref = (next(it) for _ in range(7))
    lq1_ref, lk1_ref, lq2_ref, lk2_ref, subg_ref = (next(it) for _ in range(5))
    if latent:
        cos_ref, sin_ref, ck_ref, cv_ref = (next(it) for _ in range(4))
    if kv_alias:
        next(it), next(it)
    xo_ref = next(it)
    if not latent:
        ko_ref, vo_ref = next(it), next(it)
    qkv_s, qlo_s, qhi_s, kb_s, vb_s, oh_s, o_s = (next(it) for _ in range(7))

    mod = mod_ref[...]
    sh1, sc1, g1 = mod[:, 0:D], mod[:, D:2 * D], mod[:, 2 * D:3 * D]
    n_col = (2 * QK_W + H * V_DIM) // 128
    for c in range(seq // TM):
        rs = slice(c * TM, (c + 1) * TM)
        h = _rms_mod(x_ref[rs, :], n1g_ref[...], sc1, sh1).astype(BF16)
        qkv = jnp.dot(h, wqkv_ref[...], preferred_element_type=F32)
        for cb in range(n_col):
            qkv_s[cb, rs, :] = qkv[:, cb * 128:(cb + 1) * 128]

    lam = (jnp.exp(jnp.sum(lq1_ref[...] * lk1_ref[...], axis=-1, keepdims=True))
           - jnp.exp(jnp.sum(lq2_ref[...] * lk2_ref[...], axis=-1, keepdims=True)) + lam_init)

    lane = lax.broadcasted_iota(jnp.int32, (1, 2 * HEAD_DIM), 1)
    lo = lane < HEAD_DIM
    first16 = (lane % 32) < 16
    scale = 1.0 / math.sqrt(HEAD_DIM)

    def head(hd, carry):
        qn = _half_norm(qkv_s[hd], qg_ref[...], lo)
        kn = _half_norm(qkv_s[H + hd], kg_ref[...], lo)
        vh = qkv_s[2 * H + hd]
        if latent:
            qn = _rope(qn, cos_ref[...], sin_ref[...], first16)
            kn = _rope(kn, cos_ref[...], sin_ref[...], first16)
        else:
            ko_ref[hd, 0] = kn[:, :HEAD_DIM]
            ko_ref[hd, 1] = kn[:, HEAD_DIM:]
            vo_ref[hd] = vh
        qs = qn * scale
        qlo_s[...] = jnp.where(lo, qs, 0.0).astype(BF16)
        qhi_s[...] = jnp.where(lo, 0.0, qs).astype(BF16)
        kb_s[0:seq, :] = kn.astype(BF16)
        vb_s[0:seq, :] = vh.astype(BF16)
        if latent:
            kb_s[seq:seq + PAST_LEN, :] = ck_ref[hd].astype(BF16)
            vb_s[seq:seq + PAST_LEN, :] = cv_ref[hd].astype(BF16)

        def q_block(qb, carry2):
            r0 = pl.multiple_of(qb * q_blk, q_blk)
            es, dens = [], []
            for q_s in (qlo_s, qhi_s):
                s = _dot_nt(q_s[pl.ds(r0, q_blk), :], kb_s[...])
                e = jnp.exp(s - jnp.max(s, axis=-1, keepdims=True))
                es.append(e)
                dens.append(jnp.sum(e, axis=-1, keepdims=True))
            a = (es[0] * (1.0 / dens[0]) - es[1] * (lam / dens[1])).astype(BF16)
            o = jnp.dot(a, vb_s[...], preferred_element_type=F32)
            on = o * lax.rsqrt(jnp.mean(o * o, axis=-1, keepdims=True) + RMS_EPS) * subg_ref[...] * (1.0 - lam_init)
            oh_s[hd, pl.ds(r0, q_blk), :] = on.astype(BF16)
            return carry2

        lax.fori_loop(0, seq // q_blk, q_block, 0)
        return carry

    lax.fori_loop(0, H, head, 0)

    for hd in range(H):
        o_s[:, hd * 128:(hd + 1) * 128] = oh_s[hd]
    for c in range(seq // TM):
        rs = slice(c * TM, (c + 1) * TM)
        out = jnp.dot(o_s[rs, :], wo_ref[...], preferred_element_type=F32)
        xo_ref[rs, :] = x_ref[rs, :] + g1 * out


def _attention(x_all, mod, layer, j, latent, p, rope_tabs, ck, cv, kv_acc=None):
    seq = DEC_SEQ if latent else SEQ
    nb = DEC_BATCH if latent else BATCH
    blk_off = N_P // seq if latent else 0
    n_keys = seq + PAST_LEN if latent else seq
    lam_init = 0.8 - 0.6 * math.exp(-0.3 * layer)
    row = (lambda b: 1 + b) if latent else (lambda b: 0)
    two = lambda v: jnp.concatenate([v, v])[None, :]
    vec = lambda v: v[None, :]

    args = [x_all, mod, vec(p["norm1_g"][layer]), p["attn_w_qkv"][j].astype(BF16), p["attn_w_o"][j].astype(BF16),
            two(p["attn_q_g"][j]), two(p["attn_k_g"][j]), vec(p["attn_lq1"][j]), vec(p["attn_lk1"][j]),
            vec(p["attn_lq2"][j]), vec(p["attn_lk2"][j]), vec(p["attn_sub_g"][j])]
    in_specs = [pl.BlockSpec((seq, D), lambda b: (blk_off + b, 0)), _mod_spec(layer, row), _full((1, D)),
                _full((D, 2 * QK_W + H * V_DIM)), _full((H * V_DIM, D)), _full((1, 128)), _full((1, 128)),
                _full((1, HEAD_DIM)), _full((1, HEAD_DIM)), _full((1, HEAD_DIM)), _full((1, HEAD_DIM)),
                _full((1, V_DIM))]
    out_shape = [jax.ShapeDtypeStruct((N_TOK, D), F32)]
    out_specs = [pl.BlockSpec((seq, D), lambda b: (blk_off + b, 0))]
    aliases = {0: 0}
    if latent:
        args += [rope_tabs[0], rope_tabs[1], ck, cv]
        in_specs += [_full((seq, 128)), _full((seq, 128)),
                     pl.BlockSpec((None, None, H, PAST_LEN, 128), lambda b: (b, j, 0, 0, 0)),
                     pl.BlockSpec((None, None, H, PAST_LEN, V_DIM), lambda b: (b, j, 0, 0, 0))]
    else:
        n_attn = (DEPTH + N_MIXERS - 1) // N_MIXERS
        out_shape += [jax.ShapeDtypeStruct((nb, n_attn, H, 2, seq, HEAD_DIM), F32),
                      jax.ShapeDtypeStruct((nb, n_attn, H, seq, V_DIM), F32)]
        out_specs += [pl.BlockSpec((None, None, H, 2, seq, HEAD_DIM), lambda b: (b, j, 0, 0, 0, 0)),
                      pl.BlockSpec((None, None, H, seq, V_DIM), lambda b: (b, j, 0, 0, 0))]
        if kv_acc is not None:
            aliases = {0: 0, len(args): 1, len(args) + 1: 2}
            args += list(kv_acc)
            in_specs += [pl.BlockSpec(memory_space=pl.ANY), pl.BlockSpec(memory_space=pl.ANY)]
    return pl.pallas_call(
        functools.partial(_attn_body, seq=seq, lam_init=lam_init, latent=latent, q_blk=256,
                          kv_alias=kv_acc is not None),
        grid=(nb,),
        in_specs=in_specs,
        out_specs=out_specs,
        out_shape=out_shape,
        scratch_shapes=[pltpu.VMEM(((2 * QK_W + H * V_DIM) // 128, seq, 128), F32),
                        pltpu.VMEM((seq, 128), BF16), pltpu.VMEM((seq, 128), BF16),
                        pltpu.VMEM((n_keys, 128), BF16), pltpu.VMEM((n_keys, V_DIM), BF16),
                        pltpu.VMEM((H, seq, V_DIM), BF16), pltpu.VMEM((seq, H * V_DIM), BF16)],
        input_output_aliases=aliases,
        compiler_params=_cparams(("arbitrary",)),
        name="attn_latent" if latent else "attn_prompt",
    )(*args)


def _conv_body(x_ref, mod_ref, n1g_ref, w1_ref, b1_ref, wdw_ref, bdw_ref, lng_ref, lnb_ref, w2_ref, b2_ref,
               xo_ref, upad, cv_s, *, seq):
    mod = mod_ref[...]
    sh1, sc1, g1 = mod[:, 0:D], mod[:, D:2 * D], mod[:, 2 * D:3 * D]
    zeros = jnp.zeros((HALO, D), F32)
    upad[0:HALO, :] = zeros
    upad[HALO + seq:2 * HALO + seq, :] = zeros
    for c in range(seq // TM):
        rs = slice(c * TM, (c + 1) * TM)
        h = _rms_mod(x_ref[rs, :], n1g_ref[...], sc1, sh1).astype(BF16)
        y = jnp.dot(h, w1_ref[...], preferred_element_type=F32) + b1_ref[...]
        upad[HALO + c * TM:HALO + (c + 1) * TM, :] = y[:, :D] * jax.nn.sigmoid(y[:, D:])

    pad = CONV_WIDTH // 2
    cb = 256

    def conv_rows(rb, carry):
        r0 = pl.multiple_of(rb * ROW_BLK, ROW_BLK)
        for c in range(D // cb):
            cs = slice(c * cb, (c + 1) * cb)
            win = upad[pl.ds(r0, ROW_BLK + 2 * HALO), cs]
            acc = jnp.zeros((ROW_BLK, cb), F32) + bdw_ref[:, cs]
            for k in range(CONV_WIDTH):
                off = HALO - pad + k
                acc = acc + win[off:off + ROW_BLK, :] * wdw_ref[k:k + 1, cs]
            cv_s[pl.ds(r0, ROW_BLK), cs] = acc
        return carry

    lax.fori_loop(0, seq // ROW_BLK, conv_rows, 0)

    for c in range(seq // TM):
        rs = slice(c * TM, (c + 1) * TM)
        v = cv_s[rs, :]
        mu = jnp.mean(v, axis=-1, keepdims=True)
        var = jnp.mean(jnp.square(v - mu), axis=-1, keepdims=True)
        yn = (v - mu) * lax.rsqrt(var + LN_EPS) * lng_ref[...] + lnb_ref[...]
        u2 = (yn * jax.nn.sigmoid(yn)).astype(BF16)
        out = jnp.dot(u2, w2_ref[...], preferred_element_type=F32) + b2_ref[...]
        xo_ref[rs, :] = x_ref[rs, :] + g1 * out


def _conformer(x_all, mod, layer, j, latent, p):
    seq = DEC_SEQ if latent else SEQ
    nb = DEC_BATCH if latent else BATCH
    blk_off = N_P // seq if latent else 0
    row = (lambda b: 1 + b) if latent else (lambda b: 0)
    vec = lambda v: v[None, :]
    return pl.pallas_call(
        functools.partial(_conv_body, seq=seq),
        grid=(nb,),
        in_specs=[pl.BlockSpec((seq, D), lambda b: (blk_off + b, 0)), _mod_spec(layer, row), _full((1, D)),
                  _full((D, 2 * D)), _full((1, 2 * D)), _full((CONV_WIDTH, D)), _full((1, D)), _full((1, D)),
                  _full((1, D)), _full((D, D)), _full((1, D))],
        out_specs=pl.BlockSpec((seq, D), lambda b: (blk_off + b, 0)),
        out_shape=jax.ShapeDtypeStruct((N_TOK, D), F32),
        scratch_shapes=[pltpu.VMEM((seq + 2 * HALO, D), F32), pltpu.VMEM((seq, D), F32)],
        input_output_aliases={0: 0},
        compiler_params=_cparams(("arbitrary",)),
        name="conv_latent" if latent else "conv_prompt",
    )(x_all, mod, vec(p["norm1_g"][layer]), p["conv_w_pw1"][j].astype(BF16), vec(p["conv_b_pw1"][j]),
      p["conv_w_dw"][j], vec(p["conv_b_dw"][j]), vec(p["conv_ln_g"][j]), vec(p["conv_ln_b"][j]),
      p["conv_w_pw2"][j].astype(BF16), vec(p["conv_b_pw2"][j]))


def _pool_body(x_ref, mod_ref, n1g_ref, wp_ref, ps_ref, xo_ref, hpad, p_s, *, seq):
    mod = mod_ref[...]
    sh1, sc1, g1 = mod[:, 0:D], mod[:, D:2 * D], mod[:, 2 * D:3 * D]
    zeros = jnp.zeros((HALO, D), F32)
    hpad[0:HALO, :] = zeros
    hpad[HALO + seq:2 * HALO + seq, :] = zeros
    for c in range(seq // TM):
        rs = slice(c * TM, (c + 1) * TM)
        hpad[HALO + c * TM:HALO + (c + 1) * TM, :] = _rms_mod(x_ref[rs, :], n1g_ref[...], sc1, sh1)

    def pool_rows(rb, carry):
        r0 = pl.multiple_of(rb * ROW_BLK, ROW_BLK)
        t = r0 + lax.broadcasted_iota(jnp.int32, (ROW_BLK, 1), 0)
        for g, w in enumerate(POOL_WINDOWS):
            cs = slice(g * POOL_GROUP, (g + 1) * POOL_GROUP)
            win = hpad[pl.ds(r0, ROW_BLK + 2 * HALO), cs]
            acc = jnp.zeros((ROW_BLK, POOL_GROUP), F32)
            for d in range(-(w // 2), w - w // 2):
                acc = acc + win[HALO + d:HALO + d + ROW_BLK, :]
            cnt = jnp.minimum(t + (w - w // 2), seq) - jnp.maximum(t - w // 2, 0)
            pooled = acc / cnt.astype(F32) - win[HALO:HALO + ROW_BLK, :]
            p_s[pl.ds(r0, ROW_BLK), cs] = pooled.astype(BF16)
        return carry

    lax.fori_loop(0, seq // ROW_BLK, pool_rows, 0)

    for c in range(seq // TM):
        rs = slice(c * TM, (c + 1) * TM)
        for g in range(len(POOL_WINDOWS)):
            cs = slice(g * POOL_GROUP, (g + 1) * POOL_GROUP)
            y = jnp.dot(p_s[rs, cs], wp_ref[g], preferred_element_type=F32) * ps_ref[:, cs]
            xo_ref[rs, cs] = x_ref[rs, cs] + g1[:, cs] * y


def _pool_mixer(x_all, mod, layer, j, latent, p):
    seq = DEC_SEQ if latent else SEQ
    nb = DEC_BATCH if latent else BATCH
    blk_off = N_P // seq if latent else 0
    row = (lambda b: 1 + b) if latent else (lambda b: 0)
    ng = len(POOL_WINDOWS)
    return pl.pallas_call(
        functools.partial(_pool_body, seq=seq),
        grid=(nb,),
        in_specs=[pl.BlockSpec((seq, D), lambda b: (blk_off + b, 0)), _mod_spec(layer, row), _full((1, D)),
                  _full((ng, POOL_GROUP, POOL_GROUP)), _full((1, D))],
        out_specs=pl.BlockSpec((seq, D), lambda b: (blk_off + b, 0)),
        out_shape=jax.ShapeDtypeStruct((N_TOK, D), F32),
        scratch_shapes=[pltpu.VMEM((seq + 2 * HALO, D), F32), pltpu.VMEM((seq, D), BF16)],
        input_output_aliases={0: 0},
        compiler_params=_cparams(("arbitrary",)),
        name="pool_latent" if latent else "pool_prompt",
    )(x_all, mod, p["norm1_g"][layer][None, :], p["pool_w"][j].astype(BF16), p["pool_scale"][j][None, :])


def _router_body(x_ref, mod_ref, n2g_ref, wr_ref, br_ref, h_ref, idx_ref, gate_ref, cnt_ref, cnt_s):
    t = pl.program_id(0)

    @pl.when(t == 0)
    def _():
        cnt_s[...] = jnp.zeros_like(cnt_s)

    mod = mod_ref[...]
    sh2, sc2 = mod[:, 3 * D:4 * D], mod[:, 4 * D:5 * D]
    h = _rms_mod(x_ref[...], n2g_ref[...], sc2, sh2)
    for s in range(D // 128):
        h_ref[pl.ds(s, TM, stride=8), :] = h[:, s * 128:(s + 1) * 128]
    logits = jnp.dot(h.astype(BF16), wr_ref[...], preferred_element_type=F32) + br_ref[...]
    lane = lax.broadcasted_iota(jnp.int32, logits.shape, 1)
    neg_inf = float("-inf")
    vals, idxs = [], []
    for _ in range(TOP_K):
        m = jnp.max(logits, axis=-1, keepdims=True)
        ix = jnp.min(jnp.where(logits == m, lane, 128), axis=-1, keepdims=True)
        vals.append(m)
        idxs.append(ix)
        logits = jnp.where(lane == ix, neg_inf, logits)
    es = [jnp.exp(v - vals[0]) for v in vals]
    den = es[0] + es[1] + es[2] + es[3]

    member = jnp.zeros(lane.shape, F32)
    for k in range(TOP_K):
        member = jnp.where(lane == idxs[k], 1.0, member)
    r_i = lax.broadcasted_iota(jnp.int32, (TM, TM), 0)
    c_i = lax.broadcasted_iota(jnp.int32, (TM, TM), 1)
    earlier = jnp.where(c_i < r_i, 1.0, 0.0).astype(BF16)
    before = jnp.dot(earlier, member.astype(BF16), preferred_element_type=F32) + cnt_s[...]
    cnt_s[...] = cnt_s[...] + jnp.sum(member, axis=0, keepdims=True)
    cnt_ref[...] = cnt_s[...].astype(jnp.int32)

    idx_out = jnp.zeros(lane.shape, jnp.int32)
    gate_out = jnp.zeros(lane.shape, F32)
    for k in range(TOP_K):
        rank_k = jnp.sum(jnp.where(lane == idxs[k], before, 0.0), axis=-1, keepdims=True).astype(jnp.int32)
        idx_out = jnp.where(lane == k, idxs[k], idx_out)
        idx_out = jnp.where(lane == TOP_K + k, rank_k, idx_out)
        gate_out = jnp.where(lane == k, es[k] / den, gate_out)
    idx_ref[...] = idx_out
    gate_ref[...] = gate_out


def _router(x_all, mod, layer, p):
    wr = jnp.zeros((D, 128), BF16).at[:, :E].set(p["moe_w_router"][layer].astype(BF16))
    br = jnp.full((1, 128), float("-inf"), F32).at[0, :E].set(p["moe_b_router"][layer])
    row = lambda t: _cond_row(t, TM)
    return pl.pallas_call(
        _router_body,
        grid=(N_TOK // TM,),
        in_specs=[pl.BlockSpec((TM, D), lambda t: (t, 0)), _mod_spec(layer, row), _full((1, D)),
                  _full((D, 128)), _full((1, 128))],
        out_specs=[pl.BlockSpec((TM * 8, 128), lambda t: (t, 0)), pl.BlockSpec((TM, 128), lambda t: (t, 0)),
                   pl.BlockSpec((TM, 128), lambda t: (t, 0)), pl.BlockSpec((1, 128), lambda t: (0, 0))],
        out_shape=[jax.ShapeDtypeStruct((N_TOK * 8, 128), F32), jax.ShapeDtypeStruct((N_TOK, 128), jnp.int32),
                   jax.ShapeDtypeStruct((N_TOK, 128), F32), jax.ShapeDtypeStruct((1, 128), jnp.int32)],
        scratch_shapes=[pltpu.VMEM((1, 128), F32)],
        compiler_params=_cparams(("arbitrary",)),
        name="router",
    )(x_all, mod, p["norm2_g"][layer][None, :], wr, br)


def _dispatch_plan(idx_rank, counts):
    counts = counts[0, :E]
    padded = ((counts + TM - 1) // TM) * TM
    pad_end = jnp.cumsum(padded)
    pad_start = pad_end - padded
    top_idx, rank = idx_rank[:, :TOP_K], idx_rank[:, TOP_K:2 * TOP_K]
    onehot = top_idx[:, :, None] == jnp.arange(E, dtype=jnp.int32)[None, None, :]
    dest = (jnp.sum(jnp.where(onehot, pad_start[None, None, :], 0), axis=-1) + rank).reshape(NK)
    tile_first = jnp.arange(N_TILES, dtype=jnp.int32) * TM
    tile_expert = jnp.minimum(jnp.sum(pad_end[None, :] <= tile_first[:, None], axis=1), E - 1).astype(jnp.int32)
    n_used = (pad_end[-1] // TM).astype(jnp.int32).reshape(1)
    return dest.astype(jnp.int32), tile_expert, n_used, (pad_start + counts).astype(jnp.int32), pad_end.astype(jnp.int32)


def _row_copy(src, dst, src_row, dst_row, sem):
    first = lambda row: row * 8 if isinstance(row, int) else pl.multiple_of(row * 8, 8)
    return pltpu.make_async_copy(src.at[pl.ds(first(src_row), 8)], dst.at[pl.ds(first(dst_row), 8)], sem)


def _chunked_rows(n_rows, issue_row, src, dst, sem):
    n_chunks = n_rows // GATHER_CHUNK

    def chunk_wait(slot):
        pltpu.make_async_copy(src.at[pl.ds(0, GATHER_CHUNK * 8)], dst.at[pl.ds(0, GATHER_CHUNK * 8)],
                              sem.at[slot]).wait()

    def chunk(c, carry):
        slot = c % 2

        def row(r, carry2):
            issue_row(c * GATHER_CHUNK + r, sem.at[slot])
            return carry2

        lax.fori_loop(0, GATHER_CHUNK, row, 0, unroll=8)

        @pl.when(c > 0)
        def _():
            chunk_wait(1 - slot)

        return carry

    lax.fori_loop(0, n_chunks, chunk, 0)
    chunk_wait((n_chunks - 1) % 2)


def _dispatch_body(dest_ref, fill_lo_ref, fill_hi_ref, src, dst, sem, fill_sem):
    def issue(a, s):
        _row_copy(src, dst, lax.shift_right_logical(a, 2), dest_ref[a], s).start()

    _chunked_rows(NK, issue, src, dst, sem)

    def fill_expert(e, carry):
        def start(slot, c2):
            _row_copy(src, dst, 0, slot, fill_sem).start()
            return c2

        def wait(slot, c2):
            _row_copy(src, dst, 0, slot, fill_sem).wait()
            return c2

        lax.fori_loop(fill_lo_ref[e], fill_hi_ref[e], start, 0)
        lax.fori_loop(fill_lo_ref[e], fill_hi_ref[e], wait, 0)
        return carry

    lax.fori_loop(0, E, fill_expert, 0)


def _dispatch(dest, fill_lo, fill_hi, h_tok):
    return pl.pallas_call(
        _dispatch_body,
        grid_spec=pltpu.PrefetchScalarGridSpec(
            num_scalar_prefetch=3,
            grid=(1,),
            in_specs=[pl.BlockSpec(memory_space=pl.ANY)],
            out_specs=pl.BlockSpec(memory_space=pl.ANY),
            scratch_shapes=[pltpu.SemaphoreType.DMA((2,)), pltpu.SemaphoreType.DMA],
        ),
        out_shape=jax.ShapeDtypeStruct((R_SLOTS * 8, 128), F32),
        compiler_params=_cparams(("arbitrary",)),
        name="dispatch_rows",
    )(dest, fill_lo, fill_hi, h_tok)


def _collect_body(dest_ref, src, dst, sem):
    for k in range(TOP_K):
        def issue(tok, s, k=k):
            _row_copy(src, dst, dest_ref[tok * TOP_K + k], k * N_TOK + tok, s).start()

        _chunked_rows(N_TOK, issue, src, dst, sem)


def _collect(dest, ys):
    return pl.pallas_call(
        _collect_body,
        grid_spec=pltpu.PrefetchScalarGridSpec(
            num_scalar_prefetch=1,
            grid=(1,),
            in_specs=[pl.BlockSpec(memory_space=pl.ANY)],
            out_specs=pl.BlockSpec(memory_space=pl.ANY),
            scratch_shapes=[pltpu.SemaphoreType.DMA((2,))],
        ),
        out_shape=jax.ShapeDtypeStruct((NK * 8, 128), F32),
        compiler_params=_cparams(("arbitrary",)),
        name="collect_rows",
    )(dest, ys)


def _expert_body(te_ref, nu_ref, xs_ref, wgu_ref, bgu_ref, wdn_ref, bdn_ref, ys_ref, wgu_bf, wdn_bf):
    i = pl.program_id(0)
    changed = jnp.logical_or(i == 0, te_ref[i] != te_ref[jnp.maximum(i - 1, 0)])

    @pl.when(jnp.logical_and(changed, i < nu_ref[0]))
    def _():
        for c in range(D // TM):
            rs = slice(c * TM, (c + 1) * TM)
            wgu_bf[rs, :] = wgu_ref[0, rs, :].astype(BF16)
            wdn_bf[rs, :] = wdn_ref[0, rs, :].astype(BF16)

    @pl.when(i < nu_ref[0])
    def _():
        x = jnp.concatenate([xs_ref[pl.ds(s, TM, stride=8), :].astype(BF16) for s in range(D // 128)], axis=1)
        gu = jnp.dot(x, wgu_bf[...], preferred_element_type=F32) + bgu_ref[0]
        gate = jnp.minimum(gu[:, :FF], SWIGLU_LIMIT)
        up = jnp.clip(gu[:, FF:], -SWIGLU_LIMIT, SWIGLU_LIMIT)
        glu = gate * jax.nn.sigmoid(SWIGLU_ALPHA * gate)
        act = ((up + 1.0) * glu).astype(BF16)
        y = jnp.dot(act, wdn_bf[...], preferred_element_type=F32) + bdn_ref[0]
        for s in range(D // 128):
            ys_ref[pl.ds(s, TM, stride=8), :] = y[:, s * 128:(s + 1) * 128]


def _experts(tile_expert, n_used, xs, layer, p):
    used = lambda i, nu: jnp.minimum(i, nu[0] - 1)
    return pl.pallas_call(
        _expert_body,
        grid_spec=pltpu.PrefetchScalarGridSpec(
            num_scalar_prefetch=2,
            grid=(N_TILES,),
            in_specs=[pl.BlockSpec((TM * 8, 128), lambda i, te, nu: (used(i, nu), 0)),
                      pl.BlockSpec((1, D, 2 * FF), lambda i, te, nu: (te[i], 0, 0)),
                      pl.BlockSpec((1, 1, 2 * FF), lambda i, te, nu: (te[i], 0, 0)),
                      pl.BlockSpec((1, FF, D), lambda i, te, nu: (te[i], 0, 0)),
                      pl.BlockSpec((1, 1, D), lambda i, te, nu: (te[i], 0, 0))],
            out_specs=pl.BlockSpec((TM * 8, 128), lambda i, te, nu: (used(i, nu), 0)),
            scratch_shapes=[pltpu.VMEM((D, 2 * FF), BF16), pltpu.VMEM((FF, D), BF16)],
        ),
        out_shape=jax.ShapeDtypeStruct((R_SLOTS * 8, 128), F32),
        compiler_params=_cparams(("arbitrary",)),
        name="experts",
    )(tile_expert, n_used, xs, p["moe_w_gate_up"][layer], p["moe_b_gate_up"][layer].reshape(E, 1, 2 * FF),
      p["moe_w_down"][layer], p["moe_b_down"][layer].reshape(E, 1, D))


def _combine_body(x_ref, mod_ref, yg_ref, gate_ref, xo_ref):
    gates = gate_ref[...]
    for s in range(D // 128):
        cs = slice(s * 128, (s + 1) * 128)
        f = gates[:, 0:1] * yg_ref[0, pl.ds(s, TM, stride=8), :]
        for k in range(1, TOP_K):
            f = f + gates[:, k:k + 1] * yg_ref[k, pl.ds(s, TM, stride=8), :]
        xo_ref[:, cs] = x_ref[:, cs] + mod_ref[:, 5 * D + s * 128:5 * D + (s + 1) * 128] * f


def _combine(x_all, mod, layer, yg, gates):
    row = lambda t: _cond_row(t, TM)
    return pl.pallas_call(
        _combine_body,
        grid=(N_TOK // TM,),
        in_specs=[pl.BlockSpec((TM, D), lambda t: (t, 0)), _mod_spec(layer, row),
                  pl.BlockSpec((TOP_K, TM * 8, 128), lambda t: (0, t, 0)), pl.BlockSpec((TM, 128), lambda t: (t, 0))],
        out_specs=pl.BlockSpec((TM, D), lambda t: (t, 0)),
        out_shape=jax.ShapeDtypeStruct((N_TOK, D), F32),
        input_output_aliases={0: 0},
        compiler_params=_cparams(("arbitrary",)),
        name="combine",
    )(x_all, mod, yg, gates)


def _moe(x_all, mod, layer, p):
    h_tok, idx_rank, gates, counts = _router(x_all, mod, layer, p)
    dest, tile_expert, n_used, fill_lo, fill_hi = _dispatch_plan(idx_rank, counts)
    xs = _dispatch(dest, fill_lo, fill_hi, h_tok)
    ys = _experts(tile_expert, n_used, xs, layer, p)
    yg = _collect(dest, ys).reshape(TOP_K, N_TOK * 8, 128)
    return _combine(x_all, mod, layer, yg, gates)


def _rope_tables():
    rows = DEC_SEQ // GRID_W
    r = jnp.repeat(jnp.arange(rows), GRID_W).astype(F32)
    col = jnp.tile(jnp.arange(GRID_W), rows).astype(F32)
    half = HEAD_DIM // 2
    inv = ROPE_THETA ** (-jnp.arange(0, half, 2, dtype=F32) / half)
    ang_r = r[:, None] * inv[None, :]
    ang_c = col[:, None] * inv[None, :]
    cos64 = jnp.concatenate([jnp.cos(ang_r), jnp.cos(ang_r), jnp.cos(ang_c), jnp.cos(ang_c)], axis=-1)
    sin64 = jnp.concatenate([-jnp.sin(ang_r), jnp.sin(ang_r), -jnp.sin(ang_c), jnp.sin(ang_c)], axis=-1)
    return jnp.concatenate([cos64, cos64], axis=-1), jnp.concatenate([sin64, sin64], axis=-1)


def kernel(x_prompt, x_sample, cache_k, cache_v, c, c_ctx, norm1_g, norm2_g, w_ada, b_ada, attn_w_qkv, attn_w_o, attn_q_g, attn_k_g, attn_lq1, attn_lk1, attn_lq2, attn_lk2, attn_sub_g, conv_w_pw1, conv_b_pw1, conv_w_dw, conv_b_dw, conv_ln_g, conv_ln_b, conv_w_pw2, conv_b_pw2, pool_w, pool_scale, moe_w_router, moe_b_router, moe_w_gate_up, moe_b_gate_up, moe_w_down, moe_b_down):
    p = {
        "norm1_g": norm1_g, "norm2_g": norm2_g,
        "attn_w_qkv": attn_w_qkv, "attn_w_o": attn_w_o, "attn_q_g": attn_q_g, "attn_k_g": attn_k_g,
        "attn_lq1": attn_lq1, "attn_lk1": attn_lk1, "attn_lq2": attn_lq2, "attn_lk2": attn_lk2,
        "attn_sub_g": attn_sub_g,
        "conv_w_pw1": conv_w_pw1, "conv_b_pw1": conv_b_pw1, "conv_w_dw": conv_w_dw, "conv_b_dw": conv_b_dw,
        "conv_ln_g": conv_ln_g, "conv_ln_b": conv_ln_b, "conv_w_pw2": conv_w_pw2, "conv_b_pw2": conv_b_pw2,
        "pool_w": pool_w, "pool_scale": pool_scale,
        "moe_w_router": moe_w_router, "moe_b_router": moe_b_router, "moe_w_gate_up": moe_w_gate_up,
        "moe_b_gate_up": moe_b_gate_up, "moe_w_down": moe_w_down, "moe_b_down": moe_b_down,
    }
    cond = jnp.concatenate([c_ctx[None, :], c, jnp.zeros((N_COND - 1 - DEC_BATCH, D), F32)], axis=0)
    mod = _ada_table(cond, w_ada, b_ada).reshape(DEPTH, N_COND, 1, 6 * D)
    rope_tabs = _rope_tables()
    ck = cache_k.transpose(0, 1, 2, 4, 3, 5).reshape(DEC_BATCH, -1, H, PAST_LEN, 2 * HEAD_DIM)

    x_all = jnp.concatenate([x_prompt.reshape(N_P, D), x_sample.reshape(N_S, D)], axis=0)
    kv_acc = None
    for i in range(DEPTH):
        j, kind = i // N_MIXERS, i % N_MIXERS
        if kind == 0:
            x_all, *kv_acc = _attention(x_all, mod, i, j, False, p, None, None, None, kv_acc)
            (x_all,) = _attention(x_all, mod, i, j, True, p, rope_tabs, ck, cache_v)
        elif kind == 1:
            x_all = _conformer(x_all, mod, i, j, False, p)
            x_all = _conformer(x_all, mod, i, j, True, p)
        else:
            x_all = _pool_mixer(x_all, mod, i, j, False, p)
            x_all = _pool_mixer(x_all, mod, i, j, True, p)
        x_all = _moe(x_all, mod, i, p)
    y_prompt = x_all[:N_P].reshape(BATCH, SEQ, D)
    y_sample = x_all[N_P:].reshape(DEC_BATCH, DEC_SEQ, D)
    return (y_prompt, y_sample, kv_acc[0], kv_acc[1])
```

```python
import functools
import math

import jax
import jax.numpy as jnp
from jax import lax
from jax.experimental import pallas as pl
from jax.experimental.pallas import tpu as pltpu

F32 = jnp.float32
BF16 = jnp.bfloat16

D = 1024
DEPTH = 4
BATCH, SEQ = 32, 256
DEC_BATCH, DEC_SEQ = 4, 1024
PAST_LEN = 256
GRID_W = 64
N_MIXERS = 3
H = 8
HEAD_DIM = 64
V_DIM = 2 * HEAD_DIM
QK_W = H * 2 * HEAD_DIM
ROPE_THETA = 10000.0
CONV_WIDTH = 31
POOL_WINDOWS = (2, 4, 8, 16)
POOL_GROUP = D // 4
E = 32
TOP_K = 4
FF = D
SWIGLU_ALPHA = 1.702
SWIGLU_LIMIT = 7.0
RMS_EPS = 1e-6
LN_EPS = 1e-5

N_P = BATCH * SEQ
N_S = DEC_BATCH * DEC_SEQ
N_TOK = N_P + N_S
N_COND = 8

TM = 256
NK = N_TOK * TOP_K
N_TILES = NK // TM + E
R_SLOTS = N_TILES * TM
HALO = 16
ROW_BLK = 32
assert TOP_K == 4
VMEM_LIMIT = 56 * 1024 * 1024


def _cparams(sem=None):
    return pltpu.CompilerParams(dimension_semantics=sem, vmem_limit_bytes=VMEM_LIMIT)


def _cond_row(tile, rows_per_tile):
    n_prompt_tiles = N_P // rows_per_tile
    tiles_per_seq = DEC_SEQ // rows_per_tile
    return jnp.where(tile < n_prompt_tiles, 0, 1 + (tile - n_prompt_tiles) // tiles_per_seq)


def _mod_spec(layer, row_of_step):
    return pl.BlockSpec((None, None, 1, 6 * D), lambda b, *_: (layer, row_of_step(b), 0, 0))


def _full(shape):
    nd = len(shape)
    return pl.BlockSpec(shape, lambda *_: (0,) * nd)


def _rms_mod(x, g, sc, sh):
    ms = jnp.mean(x * x, axis=-1, keepdims=True)
    return (x * lax.rsqrt(ms + RMS_EPS) * g) * (1.0 + sc) + sh


def _ada_body(cond_ref, w_ref, b_ref, o_ref):
    c = cond_ref[...]
    a = (c * jax.nn.sigmoid(c)).astype(BF16)
    o_ref[0] = jnp.dot(a, w_ref[0].astype(BF16), preferred_element_type=F32) + b_ref[0]


def _ada_table(cond, w_ada, b_ada):
    tn = 1536
    return pl.pallas_call(
        _ada_body,
        grid=(DEPTH, 6 * D // tn),
        in_specs=[
            pl.BlockSpec((N_COND, D), lambda i, j: (0, 0)),
            pl.BlockSpec((1, D, tn), lambda i, j: (i, 0, j)),
            pl.BlockSpec((1, 1, tn), lambda i, j: (i, 0, j)),
        ],
        out_specs=pl.BlockSpec((1, N_COND, tn), lambda i, j: (i, 0, j)),
        out_shape=jax.ShapeDtypeStruct((DEPTH, N_COND, 6 * D), F32),
        compiler_params=_cparams(("arbitrary", "arbitrary")),
        name="ada_table",
    )(cond, w_ada, b_ada.reshape(DEPTH, 1, 6 * D))


def _half_norm(x, g, lo):
    ss = x * x
    s_lo = jnp.sum(jnp.where(lo, ss, 0.0), axis=-1, keepdims=True)
    s_hi = jnp.sum(jnp.where(lo, 0.0, ss), axis=-1, keepdims=True)
    ms = jnp.where(lo, s_lo, s_hi) * (1.0 / HEAD_DIM)
    return x * lax.rsqrt(ms + RMS_EPS) * g


def _rope(x, cos, sin_signed, first16):
    partner = jnp.where(first16, pltpu.roll(x, 128 - 16, axis=1), pltpu.roll(x, 16, axis=1))
    return x * cos + partner * sin_signed


def _dot_nt(a, b):
    return lax.dot_general(a, b, (((1,), (1,)), ((), ())), preferred_element_type=F32)


def _attn_body(*refs, seq, lam_init, latent, q_blk, kv_alias):
    it = iter(refs)
    x_ref, mod_ref, n1g_ref, wqkv_ref, wo_ref, qg_ref, kg_ref = (next(it) for _ in range(7))
    lq1_ref, lk1_ref, lq2_ref, lk2_ref, subg_ref = (next(it) for _ in range(5))
    if latent:
        cos_ref, sin_ref, ck_ref, cv_ref = (next(it) for _ in range(4))
    if kv_alias:
        next(it), next(it)
    xo_ref = next(it)
    if not latent:
        ko_ref, vo_ref = next(it), next(it)
    qkv_s, qlo_s, qhi_s, kb_s, vb_s, oh_s, o_s = (next(it) for _ in range(7))

    mod = mod_ref[...]
    sh1, sc1, g1 = mod[:, 0:D], mod[:, D:2 * D], mod[:, 2 * D:3 * D]
    n_col = (2 * QK_W + H * V_DIM) // 128
    for c in range(seq // TM):
        rs = slice(c * TM, (c + 1) * TM)
        h = _rms_mod(x_ref[rs, :], n1g_ref[...], sc1, sh1).astype(BF16)
        qkv = jnp.dot(h, wqkv_ref[...], preferred_element_type=F32)
        for cb in range(n_col):
            qkv_s[cb, rs, :] = qkv[:, cb * 128:(cb + 1) * 128]

    lam = (jnp.exp(jnp.sum(lq1_ref[...] * lk1_ref[...], axis=-1, keepdims=True))
           - jnp.exp(jnp.sum(lq2_ref[...] * lk2_ref[...], axis=-1, keepdims=True)) + lam_init)

    lane = lax.broadcasted_iota(jnp.int32, (1, 2 * HEAD_DIM), 1)
    lo = lane < HEAD_DIM
    first16 = (lane % 32) < 16
    scale = 1.0 / math.sqrt(HEAD_DIM)

    def head(hd, carry):
        qn = _half_norm(qkv_s[hd], qg_ref[...], lo)
        kn = _half_norm(qkv_s[H + hd], kg_ref[...], lo)
        vh = qkv_s[2 * H + hd]
        if latent:
            qn = _rope(qn, cos_ref[...], sin_ref[...], first16)
            kn = _rope(kn, cos_ref[...], sin_ref[...], first16)
        else:
            ko_ref[hd, 0] = kn[:, :HEAD_DIM]
            ko_ref[hd, 1] = kn[:, HEAD_DIM:]
            vo_ref[hd] = vh
        qs = qn * scale
        qlo_s[...] = jnp.where(lo, qs, 0.0).astype(BF16)
        qhi_s[...] = jnp.where(lo, 0.0, qs).astype(BF16)
        kb_s[0:seq, :] = kn.astype(BF16)
        vb_s[0:seq, :] = vh.astype(BF16)
        if latent:
            kb_s[seq:seq + PAST_LEN, :] = ck_ref[hd].astype(BF16)
            vb_s[seq:seq + PAST_LEN, :] = cv_ref[hd].astype(BF16)

        def q_block(qb, carry2):
            r0 = pl.multiple_of(qb * q_blk, q_blk)
            es, dens = [], []
            for q_s in (qlo_s, qhi_s):
                s = _dot_nt(q_s[pl.ds(r0, q_blk), :], kb_s[...])
                e = jnp.exp(s - jnp.max(s, axis=-1, keepdims=True))
                es.append(e)
                dens.append(jnp.sum(e, axis=-1, keepdims=True))
            a = (es[0] * (1.0 / dens[0]) - es[1] * (lam / dens[1])).astype(BF16)
            o = jnp.dot(a, vb_s[...], preferred_element_type=F32)
            on = o * lax.rsqrt(jnp.mean(o * o, axis=-1, keepdims=True) + RMS_EPS) * subg_ref[...] * (1.0 - lam_init)
            oh_s[hd, pl.ds(r0, q_blk), :] = on.astype(BF16)
            return carry2

        lax.fori_loop(0, seq // q_blk, q_block, 0)
        return carry

    lax.fori_loop(0, H, head, 0)

    for hd in range(H):
        o_s[:, hd * 128:(hd + 1) * 128] = oh_s[hd]
    for c in range(seq // TM):
        rs = slice(c * TM, (c + 1) * TM)
        out = jnp.dot(o_s[rs, :], wo_ref[...], preferred_element_type=F32)
        xo_ref[rs, :] = x_ref[rs, :] + g1 * out


def _attention(x_all, mod, layer, j, latent, p, rope_tabs, ck, cv, kv_acc=None):
    seq = DEC_SEQ if latent else SEQ
    nb = DEC_BATCH if latent else BATCH
    blk_off = N_P // seq if latent else 0
    n_keys = seq + PAST_LEN if latent else seq
    lam_init = 0.8 - 0.6 * math.exp(-0.3 * layer)
    row = (lambda b: 1 + b) if latent else (lambda b: 0)
    two = lambda v: jnp.concatenate([v, v])[None, :]
    vec = lambda v: v[None, :]

    args = [x_all, mod, vec(p["norm1_g"][layer]), p["attn_w_qkv"][j].astype(BF16), p["attn_w_o"][j].astype(BF16),
            two(p["attn_q_g"][j]), two(p["attn_k_g"][j]), vec(p["attn_lq1"][j]), vec(p["attn_lk1"][j]),
            vec(p["attn_lq2"][j]), vec(p["attn_lk2"][j]), vec(p["attn_sub_g"][j])]
    in_specs = [pl.BlockSpec((seq, D), lambda b: (blk_off + b, 0)), _mod_spec(layer, row), _full((1, D)),
                _full((D, 2 * QK_W + H * V_DIM)), _full((H * V_DIM, D)), _full((1, 128)), _full((1, 128)),
                _full((1, HEAD_DIM)), _full((1, HEAD_DIM)), _full((1, HEAD_DIM)), _full((1, HEAD_DIM)),
                _full((1, V_DIM))]
    out_shape = [jax.ShapeDtypeStruct((N_TOK, D), F32)]
    out_specs = [pl.BlockSpec((seq, D), lambda b: (blk_off + b, 0))]
    aliases = {0: 0}
    if latent:
        args += [rope_tabs[0], rope_tabs[1], ck, cv]
        in_specs += [_full((seq, 128)), _full((seq, 128)),
                     pl.BlockSpec((None, None, H, PAST_LEN, 128), lambda b: (b, j, 0, 0, 0)),
                     pl.BlockSpec((None, None, H, PAST_LEN, V_DIM), lambda b: (b, j, 0, 0, 0))]
    else:
        n_attn = (DEPTH + N_MIXERS - 1) // N_MIXERS
        out_shape += [jax.ShapeDtypeStruct((nb, n_attn, H, 2, seq, HEAD_DIM), F32),
                      jax.ShapeDtypeStruct((nb, n_attn, H, seq, V_DIM), F32)]
        out_specs += [pl.BlockSpec((None, None, H, 2, seq, HEAD_DIM), lambda b: (b, j, 0, 0, 0, 0)),
                      pl.BlockSpec((None, None, H, seq, V_DIM), lambda b: (b, j, 0, 0, 0))]
        if kv_acc is not None:
            aliases = {0: 0, len(args): 1, len(args) + 1: 2}
            args += list(kv_acc)
            in_specs += [pl.BlockSpec(memory_space=pl.ANY), pl.BlockSpec(memory_space=pl.ANY)]
    return pl.pallas_call(
        functools.partial(_attn_body, seq=seq, lam_init=lam_init, latent=latent, q_blk=256,
                          kv_alias=kv_acc is not None),
        grid=(nb,),
        in_specs=in_specs,
        out_specs=out_specs,
        out_shape=out_shape,
        scratch_shapes=[pltpu.VMEM(((2 * QK_W + H * V_DIM) // 128, seq, 128), F32),
                        pltpu.VMEM((seq, 128), BF16), pltpu.VMEM((seq, 128), BF16),
                        pltpu.VMEM((n_keys, 128), BF16), pltpu.VMEM((n_keys, V_DIM), BF16),
                        pltpu.VMEM((H, seq, V_DIM), BF16), pltpu.VMEM((seq, H * V_DIM), BF16)],
        input_output_aliases=aliases,
        compiler_params=_cparams(("arbitrary",)),
        name="attn_latent" if latent else "attn_prompt",
    )(*args)


def _conv_body(x_ref, mod_ref, n1g_ref, w1_ref, b1_ref, wdw_ref, bdw_ref, lng_ref, lnb_ref, w2_ref, b2_ref,
               xo_ref, upad, cv_s, *, seq):
    mod = mod_ref[...]
    sh1, sc1, g1 = mod[:, 0:D], mod[:, D:2 * D], mod[:, 2 * D:3 * D]
    zeros = jnp.zeros((HALO, D), F32)
    upad[0:HALO, :] = zeros
    upad[HALO + seq:2 * HALO + seq, :] = zeros
    for c in range(seq // TM):
        rs = slice(c * TM, (c + 1) * TM)
        h = _rms_mod(x_ref[rs, :], n1g_ref[...], sc1, sh1).astype(BF16)
        y = jnp.dot(h, w1_ref[...], preferred_element_type=F32) + b1_ref[...]
        upad[HALO + c * TM:HALO + (c + 1) * TM, :] = y[:, :D] * jax.nn.sigmoid(y[:, D:])

    pad = CONV_WIDTH // 2
    cb = 256

    def conv_rows(rb, carry):
        r0 = pl.multiple_of(rb * ROW_BLK, ROW_BLK)
        for c in range(D // cb):
            cs = slice(c * cb, (c + 1) * cb)
            win = upad[pl.ds(r0, ROW_BLK + 2 * HALO), cs]
            acc = jnp.zeros((ROW_BLK, cb), F32) + bdw_ref[:, cs]
            for k in range(CONV_WIDTH):
                off = HALO - pad + k
                acc = acc + win[off:off + ROW_BLK, :] * wdw_ref[k:k + 1, cs]
            cv_s[pl.ds(r0, ROW_BLK), cs] = acc
        return carry

    lax.fori_loop(0, seq // ROW_BLK, conv_rows, 0)

    for c in range(seq // TM):
        rs = slice(c * TM, (c + 1) * TM)
        v = cv_s[rs, :]
        mu = jnp.mean(v, axis=-1, keepdims=True)
        var = jnp.mean(jnp.square(v - mu), axis=-1, keepdims=True)
        yn = (v - mu) * lax.rsqrt(var + LN_EPS) * lng_ref[...] + lnb_ref[...]
        u2 = (yn * jax.nn.sigmoid(yn)).astype(BF16)
        out = jnp.dot(u2, w2_ref[...], preferred_element_type=F32) + b2_ref[...]
        xo_ref[rs, :] = x_ref[rs, :] + g1 * out


def _conformer(x_all, mod, layer, j, latent, p):
    seq = DEC_SEQ if latent else SEQ
    nb = DEC_BATCH if latent else BATCH
    blk_off = N_P // seq if latent else 0
    row = (lambda b: 1 + b) if latent else (lambda b: 0)
    vec = lambda v: v[None, :]
    return pl.pallas_call(
        functools.partial(_conv_body, seq=seq),
        grid=(nb,),
        in_specs=[pl.BlockSpec((seq, D), lambda b: (blk_off + b, 0)), _mod_spec(layer, row), _full((1, D)),
                  _full((D, 2 * D)), _full((1, 2 * D)), _full((CONV_WIDTH, D)), _full((1, D)), _full((1, D)),
                  _full((1, D)), _full((D, D)), _full((1, D))],
        out_specs=pl.BlockSpec((seq, D), lambda b: (blk_off + b, 0)),
        out_shape=jax.ShapeDtypeStruct((N_TOK, D), F32),
        scratch_shapes=[pltpu.VMEM((seq + 2 * HALO, D), F32), pltpu.VMEM((seq, D), F32)],
        input_output_aliases={0: 0},
        compiler_params=_cparams(("arbitrary",)),
        name="conv_latent" if latent else "conv_prompt",
    )(x_all, mod, vec(p["norm1_g"][layer]), p["conv_w_pw1"][j].astype(BF16), vec(p["conv_b_pw1"][j]),
      p["conv_w_dw"][j], vec(p["conv_b_dw"][j]), vec(p["conv_ln_g"][j]), vec(p["conv_ln_b"][j]),
      p["conv_w_pw2"][j].astype(BF16), vec(p["conv_b_pw2"][j]))


def _pool_body(x_ref, mod_ref, n1g_ref, wp_ref, ps_ref, xo_ref, hpad, p_s, *, seq):
    mod = mod_ref[...]
    sh1, sc1, g1 = mod[:, 0:D], mod[:, D:2 * D], mod[:, 2 * D:3 * D]
    zeros = jnp.zeros((HALO, D), F32)
    hpad[0:HALO, :] = zeros
    hpad[HALO + seq:2 * HALO + seq, :] = zeros
    for c in range(seq // TM):
        rs = slice(c * TM, (c + 1) * TM)
        hpad[HALO + c * TM:HALO + (c + 1) * TM, :] = _rms_mod(x_ref[rs, :], n1g_ref[...], sc1, sh1)

    def pool_rows(rb, carry):
        r0 = pl.multiple_of(rb * ROW_BLK, ROW_BLK)
        t = r0 + lax.broadcasted_iota(jnp.int32, (ROW_BLK, 1), 0)
        for g, w in enumerate(POOL_WINDOWS):
            cs = slice(g * POOL_GROUP, (g + 1) * POOL_GROUP)
            win = hpad[pl.ds(r0, ROW_BLK + 2 * HALO), cs]
            acc = jnp.zeros((ROW_BLK, POOL_GROUP), F32)
            for d in range(-(w // 2), w - w // 2):
                acc = acc + win[HALO + d:HALO + d + ROW_BLK, :]
            cnt = jnp.minimum(t + (w - w // 2), seq) - jnp.maximum(t - w // 2, 0)
            pooled = acc / cnt.astype(F32) - win[HALO:HALO + ROW_BLK, :]
            p_s[pl.ds(r0, ROW_BLK), cs] = pooled.astype(BF16)
        return carry

    lax.fori_loop(0, seq // ROW_BLK, pool_rows, 0)

    for c in range(seq // TM):
        rs = slice(c * TM, (c + 1) * TM)
        for g in range(len(POOL_WINDOWS)):
            cs = slice(g * POOL_GROUP, (g + 1) * POOL_GROUP)
            y = jnp.dot(p_s[rs, cs], wp_ref[g], preferred_element_type=F32) * ps_ref[:, cs]
            xo_ref[rs, cs] = x_ref[rs, cs] + g1[:, cs] * y


def _pool_mixer(x_all, mod, layer, j, latent, p):
    seq = DEC_SEQ if latent else SEQ
    nb = DEC_BATCH if latent else BATCH
    blk_off = N_P // seq if latent else 0
    row = (lambda b: 1 + b) if latent else (lambda b: 0)
    ng = len(POOL_WINDOWS)
    return pl.pallas_call(
        functools.partial(_pool_body, seq=seq),
        grid=(nb,),
        in_specs=[pl.BlockSpec((seq, D), lambda b: (blk_off + b, 0)), _mod_spec(layer, row), _full((1, D)),
                  _full((ng, POOL_GROUP, POOL_GROUP)), _full((1, D))],
        out_specs=pl.BlockSpec((seq, D), lambda b: (blk_off + b, 0)),
        out_shape=jax.ShapeDtypeStruct((N_TOK, D), F32),
        scratch_shapes=[pltpu.VMEM((seq + 2 * HALO, D), F32), pltpu.VMEM((seq, D), BF16)],
        input_output_aliases={0: 0},
        compiler_params=_cparams(("arbitrary",)),
        name="pool_latent" if latent else "pool_prompt",
    )(x_all, mod, p["norm1_g"][layer][None, :], p["pool_w"][j].astype(BF16), p["pool_scale"][j][None, :])


def _router_body(x_ref, mod_ref, n2g_ref, wr_ref, br_ref, h_ref, idx_ref, gate_ref, cnt_ref, cnt_s):
    t = pl.program_id(0)

    @pl.when(t == 0)
    def _():
        cnt_s[...] = jnp.zeros_like(cnt_s)

    mod = mod_ref[...]
    sh2, sc2 = mod[:, 3 * D:4 * D], mod[:, 4 * D:5 * D]
    h = _rms_mod(x_ref[...], n2g_ref[...], sc2, sh2)
    for s in range(D // 128):
        h_ref[pl.ds(s, TM, stride=8), :] = h[:, s * 128:(s + 1) * 128]
    logits = jnp.dot(h.astype(BF16), wr_ref[...], preferred_element_type=F32) + br_ref[...]
    lane = lax.broadcasted_iota(jnp.int32, logits.shape, 1)
    neg_inf = float("-inf")
    vals, idxs = [], []
    for _ in range(TOP_K):
        m = jnp.max(logits, axis=-1, keepdims=True)
        ix = jnp.min(jnp.where(logits == m, lane, 128), axis=-1, keepdims=True)
        vals.append(m)
        idxs.append(ix)
        logits = jnp.where(lane == ix, neg_inf, logits)
    es = [jnp.exp(v - vals[0]) for v in vals]
    den = es[0] + es[1] + es[2] + es[3]

    member = jnp.zeros(lane.shape, F32)
    for k in range(TOP_K):
        member = jnp.where(lane == idxs[k], 1.0, member)
    r_i = lax.broadcasted_iota(jnp.int32, (TM, TM), 0)
    c_i = lax.broadcasted_iota(jnp.int32, (TM, TM), 1)
    earlier = jnp.where(c_i < r_i, 1.0, 0.0).astype(BF16)
    before = jnp.dot(earlier, member.astype(BF16), preferred_element_type=F32) + cnt_s[...]
    cnt_s[...] = cnt_s[...] + jnp.sum(member, axis=0, keepdims=True)
    cnt_ref[...] = cnt_s[...].astype(jnp.int32)

    idx_out = jnp.zeros(lane.shape, jnp.int32)
    gate_out = jnp.zeros(lane.shape, F32)
    for k in range(TOP_K):
        rank_k = jnp.sum(jnp.where(lane == idxs[k], before, 0.0), axis=-1, keepdims=True).astype(jnp.int32)
        idx_out = jnp.where(lane == k, idxs[k], idx_out)
        idx_out = jnp.where(lane == TOP_K + k, rank_k, idx_out)
        gate_out = jnp.where(lane == k, es[k] / den, gate_out)
    idx_ref[...] = idx_out
    gate_ref[...] = gate_out


def _router(x_all, mod, layer, p):
    wr = jnp.zeros((D, 128), BF16).at[:, :E].set(p["moe_w_router"][layer].astype(BF16))
    br = jnp.full((1, 128), float("-inf"), F32).at[0, :E].set(p["moe_b_router"][layer])
    row = lambda t: _cond_row(t, TM)
    return pl.pallas_call(
        _router_body,
        grid=(N_TOK // TM,),
        in_specs=[pl.BlockSpec((TM, D), lambda t: (t, 0)), _mod_spec(layer, row), _full((1, D)),
                  _full((D, 128)), _full((1, 128))],
        out_specs=[pl.BlockSpec((TM * 8, 128), lambda t: (t, 0)), pl.BlockSpec((TM, 128), lambda t: (t, 0)),
                   pl.BlockSpec((TM, 128), lambda t: (t, 0)), pl.BlockSpec((1, 128), lambda t: (0, 0))],
        out_shape=[jax.ShapeDtypeStruct((N_TOK * 8, 128), F32), jax.ShapeDtypeStruct((N_TOK, 128), jnp.int32),
                   jax.ShapeDtypeStruct((N_TOK, 128), F32), jax.ShapeDtypeStruct((1, 128), jnp.int32)],
        scratch_shapes=[pltpu.VMEM((1, 128), F32)],
        compiler_params=_cparams(("arbitrary",)),
        name="router",
    )(x_all, mod, p["norm2_g"][layer][None, :], wr, br)


def _dispatch_plan(idx_rank, counts):
    counts = counts[0, :E]
    padded = ((counts + TM - 1) // TM) * TM
    pad_end = jnp.cumsum(padded)
    pad_start = pad_end - padded
    top_idx, rank = idx_rank[:, :TOP_K], idx_rank[:, TOP_K:2 * TOP_K]
    onehot = top_idx[:, :, None] == jnp.arange(E, dtype=jnp.int32)[None, None, :]
    dest = (jnp.sum(jnp.where(onehot, pad_start[None, None, :], 0), axis=-1) + rank).reshape(NK)
    tile_first = jnp.arange(N_TILES, dtype=jnp.int32) * TM
    tile_expert = jnp.minimum(jnp.sum(pad_end[None, :] <= tile_first[:, None], axis=1), E - 1).astype(jnp.int32)
    n_used = (pad_end[-1] // TM).astype(jnp.int32).reshape(1)
    return dest.astype(jnp.int32), tile_expert, n_used, (pad_start + counts).astype(jnp.int32), pad_end.astype(jnp.int32)


def _tile_rows(row):
    return pl.ds(row * 8 if isinstance(row, int) else pl.multiple_of(row * 8, 8), 8)


def _dispatch_body(dest_ref, fill_lo_ref, fill_hi_ref, h_ref, xs_hbm, sem, fill_sem):
    i = pl.program_id(0)
    base = i * (TM * TOP_K)

    def token(t, carry):
        src = h_ref.at[_tile_rows(t)]
        for k in range(TOP_K):
            pltpu.make_async_copy(src, xs_hbm.at[_tile_rows(dest_ref[base + t * TOP_K + k])], sem).start()
        return carry

    lax.fori_loop(0, TM, token, 0, unroll=4)

    @pl.when(i == 0)
    def _():
        def fill_expert(e, carry):
            def start(slot, c2):
                pltpu.make_async_copy(h_ref.at[_tile_rows(0)], xs_hbm.at[_tile_rows(slot)], fill_sem).start()
                return c2

            def wait(slot, c2):
                pltpu.make_async_copy(h_ref.at[_tile_rows(0)], xs_hbm.at[_tile_rows(slot)], fill_sem).wait()
                return c2

            lax.fori_loop(fill_lo_ref[e], fill_hi_ref[e], start, 0)
            lax.fori_loop(fill_lo_ref[e], fill_hi_ref[e], wait, 0)
            return carry

        lax.fori_loop(0, E, fill_expert, 0)

    n = TM * TOP_K * 8
    pltpu.make_async_copy(xs_hbm.at[pl.ds(0, n)], xs_hbm.at[pl.ds(0, n)], sem).wait()


def _dispatch(dest, fill_lo, fill_hi, h_tok):
    return pl.pallas_call(
        _dispatch_body,
        grid_spec=pltpu.PrefetchScalarGridSpec(
            num_scalar_prefetch=3,
            grid=(N_TOK // TM,),
            in_specs=[pl.BlockSpec((TM * 8, 128), lambda i, *_: (i, 0))],
            out_specs=pl.BlockSpec(memory_space=pl.ANY),
            scratch_shapes=[pltpu.SemaphoreType.DMA, pltpu.SemaphoreType.DMA],
        ),
        out_shape=jax.ShapeDtypeStruct((R_SLOTS * 8, 128), F32),
        compiler_params=_cparams(("arbitrary",)),
        name="dispatch_rows",
    )(dest, fill_lo, fill_hi, h_tok)


def _expert_body(te_ref, nu_ref, xs_ref, wgu_ref, bgu_ref, wdn_ref, bdn_ref, ys_ref, wgu_bf, wdn_bf):
    i = pl.program_id(0)
    changed = jnp.logical_or(i == 0, te_ref[i] != te_ref[jnp.maximum(i - 1, 0)])

    @pl.when(jnp.logical_and(changed, i < nu_ref[0]))
    def _():
        for c in range(D // TM):
            rs = slice(c * TM, (c + 1) * TM)
            wgu_bf[rs, :] = wgu_ref[0, rs, :].astype(BF16)
            wdn_bf[rs, :] = wdn_ref[0, rs, :].astype(BF16)

    @pl.when(i < nu_ref[0])
    def _():
        x = jnp.concatenate([xs_ref[pl.ds(s, TM, stride=8), :].astype(BF16) for s in range(D // 128)], axis=1)
        gu = jnp.dot(x, wgu_bf[...], preferred_element_type=F32) + bgu_ref[0]
        gate = jnp.minimum(gu[:, :FF], SWIGLU_LIMIT)
        up = jnp.clip(gu[:, FF:], -SWIGLU_LIMIT, SWIGLU_LIMIT)
        glu = gate * jax.nn.sigmoid(SWIGLU_ALPHA * gate)
        act = ((up + 1.0) * glu).astype(BF16)
        y = jnp.dot(act, wdn_bf[...], preferred_element_type=F32) + bdn_ref[0]
        for s in range(D // 128):
            ys_ref[pl.ds(s, TM, stride=8), :] = y[:, s * 128:(s + 1) * 128]


def _experts(tile_expert, n_used, xs, layer, p):
    used = lambda i, nu: jnp.minimum(i, nu[0] - 1)
    return pl.pallas_call(
        _expert_body,
        grid_spec=pltpu.PrefetchScalarGridSpec(
            num_scalar_prefetch=2,
            grid=(N_TILES,),
            in_specs=[pl.BlockSpec((TM * 8, 128), lambda i, te, nu: (used(i, nu), 0)),
                      pl.BlockSpec((None, 1, D, 2 * FF), lambda i, te, nu: (layer, te[i], 0, 0)),
                      pl.BlockSpec((None, 1, 1, 2 * FF), lambda i, te, nu: (layer, te[i], 0, 0)),
                      pl.BlockSpec((None, 1, FF, D), lambda i, te, nu: (layer, te[i], 0, 0)),
                      pl.BlockSpec((None, 1, 1, D), lambda i, te, nu: (layer, te[i], 0, 0))],
            out_specs=pl.BlockSpec((TM * 8, 128), lambda i, te, nu: (used(i, nu), 0)),
            scratch_shapes=[pltpu.VMEM((D, 2 * FF), BF16), pltpu.VMEM((FF, D), BF16)],
        ),
        out_shape=jax.ShapeDtypeStruct((R_SLOTS * 8, 128), F32),
        compiler_params=_cparams(("arbitrary",)),
        name="experts",
    )(tile_expert, n_used, xs, p["moe_w_gate_up"], p["moe_b_gate_up"].reshape(DEPTH, E, 1, 2 * FF),
      p["moe_w_down"], p["moe_b_down"].reshape(DEPTH, E, 1, D))


def _combine_body(dest_ref, x_ref, mod_ref, gate_ref, ys_hbm, xo_ref, yg_s, sem):
    i = pl.program_id(0)

    def fetch(tile, slot):
        base = tile * (TM * TOP_K)

        def token(t, carry):
            for k in range(TOP_K):
                pltpu.make_async_copy(ys_hbm.at[_tile_rows(dest_ref[base + t * TOP_K + k])],
                                      yg_s.at[slot, k, _tile_rows(t)], sem.at[slot]).start()
            return carry

        lax.fori_loop(0, TM, token, 0, unroll=4)

    @pl.when(i == 0)
    def _():
        fetch(0, 0)

    @pl.when(i + 1 < pl.num_programs(0))
    def _():
        fetch(i + 1, (i + 1) % 2)

    slot = i % 2
    pltpu.make_async_copy(yg_s.at[slot], yg_s.at[slot], sem.at[slot]).wait()
    gates = gate_ref[...]
    for s in range(D // 128):
        cs = slice(s * 128, (s + 1) * 128)
        f = gates[:, 0:1] * yg_s[slot, 0, pl.ds(s, TM, stride=8), :]
        for k in range(1, TOP_K):
            f = f + gates[:, k:k + 1] * yg_s[slot, k, pl.ds(s, TM, stride=8), :]
        xo_ref[:, cs] = x_ref[:, cs] + mod_ref[:, 5 * D + s * 128:5 * D + (s + 1) * 128] * f


def _combine(x_all, mod, layer, dest, ys, gates):
    row = lambda t: _cond_row(t, TM)
    return pl.pallas_call(
        _combine_body,
        grid_spec=pltpu.PrefetchScalarGridSpec(
            num_scalar_prefetch=1,
            grid=(N_TOK // TM,),
            in_specs=[pl.BlockSpec((TM, D), lambda t, *_: (t, 0)), _mod_spec(layer, row),
                      pl.BlockSpec((TM, 128), lambda t, *_: (t, 0)), pl.BlockSpec(memory_space=pl.ANY)],
            out_specs=pl.BlockSpec((TM, D), lambda t, *_: (t, 0)),
            scratch_shapes=[pltpu.VMEM((2, TOP_K, TM * 8, 128), F32), pltpu.SemaphoreType.DMA((2,))],
        ),
        out_shape=jax.ShapeDtypeStruct((N_TOK, D), F32),
        input_output_aliases={1: 0},
        compiler_params=_cparams(("arbitrary",)),
        name="combine",
    )(dest, x_all, mod, gates, ys)


def _moe(x_all, mod, layer, p):
    h_tok, idx_rank, gates, counts = _router(x_all, mod, layer, p)
    dest, tile_expert, n_used, fill_lo, fill_hi = _dispatch_plan(idx_rank, counts)
    xs = _dispatch(dest, fill_lo, fill_hi, h_tok)
    ys = _experts(tile_expert, n_used, xs, layer, p)
    return _combine(x_all, mod, layer, dest, ys, gates)


def _rope_tables():
    rows = DEC_SEQ // GRID_W
    r = jnp.repeat(jnp.arange(rows), GRID_W).astype(F32)
    col = jnp.tile(jnp.arange(GRID_W), rows).astype(F32)
    half = HEAD_DIM // 2
    inv = ROPE_THETA ** (-jnp.arange(0, half, 2, dtype=F32) / half)
    ang_r = r[:, None] * inv[None, :]
    ang_c = col[:, None] * inv[None, :]
    cos64 = jnp.concatenate([jnp.cos(ang_r), jnp.cos(ang_r), jnp.cos(ang_c), jnp.cos(ang_c)], axis=-1)
    sin64 = jnp.concatenate([-jnp.sin(ang_r), jnp.sin(ang_r), -jnp.sin(ang_c), jnp.sin(ang_c)], axis=-1)
    return jnp.concatenate([cos64, cos64], axis=-1), jnp.concatenate([sin64, sin64], axis=-1)


def kernel(x_prompt, x_sample, cache_k, cache_v, c, c_ctx, norm1_g, norm2_g, w_ada, b_ada, attn_w_qkv, attn_w_o, attn_q_g, attn_k_g, attn_lq1, attn_lk1, attn_lq2, attn_lk2, attn_sub_g, conv_w_pw1, conv_b_pw1, conv_w_dw, conv_b_dw, conv_ln_g, conv_ln_b, conv_w_pw2, conv_b_pw2, pool_w, pool_scale, moe_w_router, moe_b_router, moe_w_gate_up, moe_b_gate_up, moe_w_down, moe_b_down):
    p = {
        "norm1_g": norm1_g, "norm2_g": norm2_g,
        "attn_w_qkv": attn_w_qkv, "attn_w_o": attn_w_o, "attn_q_g": attn_q_g, "attn_k_g": attn_k_g,
        "attn_lq1": attn_lq1, "attn_lk1": attn_lk1, "attn_lq2": attn_lq2, "attn_lk2": attn_lk2,
        "attn_sub_g": attn_sub_g,
        "conv_w_pw1": conv_w_pw1, "conv_b_pw1": conv_b_pw1, "conv_w_dw": conv_w_dw, "conv_b_dw": conv_b_dw,
        "conv_ln_g": conv_ln_g, "conv_ln_b": conv_ln_b, "conv_w_pw2": conv_w_pw2, "conv_b_pw2": conv_b_pw2,
        "pool_w": pool_w, "pool_scale": pool_scale,
        "moe_w_router": moe_w_router, "moe_b_router": moe_b_router, "moe_w_gate_up": moe_w_gate_up,
        "moe_b_gate_up": moe_b_gate_up, "moe_w_down": moe_w_down, "moe_b_down": moe_b_down,
    }
    cond = jnp.concatenate([c_ctx[None, :], c, jnp.zeros((N_COND - 1 - DEC_BATCH, D), F32)], axis=0)
    mod = _ada_table(cond, w_ada, b_ada).reshape(DEPTH, N_COND, 1, 6 * D)
    rope_tabs = _rope_tables()
    ck = cache_k.transpose(0, 1, 2, 4, 3, 5).reshape(DEC_BATCH, -1, H, PAST_LEN, 2 * HEAD_DIM)

    x_all = jnp.concatenate([x_prompt.reshape(N_P, D), x_sample.reshape(N_S, D)], axis=0)
    kv_acc = None
    for i in range(DEPTH):
        j, kind = i // N_MIXERS, i % N_MIXERS
        if kind == 0:
            x_all, *kv_acc = _attention(x_all, mod, i, j, False, p, None, None, None, kv_acc)
            (x_all,) = _attention(x_all, mod, i, j, True, p, rope_tabs, ck, cache_v)
        elif kind == 1:
            x_all = _conformer(x_all, mod, i, j, False, p)
            x_all = _conformer(x_all, mod, i, j, True, p)
        else:
            x_all = _pool_mixer(x_all, mod, i, j, False, p)
            x_all = _pool_mixer(x_all, mod, i, j, True, p)
        x_all = _moe(x_all, mod, i, p)
    y_prompt = x_all[:N_P].reshape(BATCH, SEQ, D)
    y_sample = x_all[N_P:].reshape(DEC_BATCH, DEC_SEQ, D)
    return (y_prompt, y_sample, kv_acc[0], kv_acc[1])
```

```python
import functools
import math

import jax
import jax.numpy as jnp
from jax import lax
from jax.experimental import pallas as pl
from jax.experimental.pallas import tpu as pltpu

F32 = jnp.float32
BF16 = jnp.bfloat16

D = 1024
DEPTH = 4
BATCH, SEQ = 32, 256
DEC_BATCH, DEC_SEQ = 4, 1024
PAST_LEN = 256
GRID_W = 64
N_MIXERS = 3
H = 8
HEAD_DIM = 64
V_DIM = 2 * HEAD_DIM
QK_W = H * 2 * HEAD_DIM
ROPE_THETA = 10000.0
CONV_WIDTH = 31
POOL_WINDOWS = (2, 4, 8, 16)
POOL_GROUP = D // 4
E = 32
TOP_K = 4
FF = D
SWIGLU_ALPHA = 1.702
SWIGLU_LIMIT = 7.0
RMS_EPS = 1e-6
LN_EPS = 1e-5

N_P = BATCH * SEQ
N_S = DEC_BATCH * DEC_SEQ
N_TOK = N_P + N_S
N_COND = 8

TM = 256
NK = N_TOK * TOP_K
N_TILES = NK // TM + E
R_SLOTS = N_TILES * TM
HALO = 16
ROW_BLK = 32
assert TOP_K == 4
VMEM_LIMIT = 56 * 1024 * 1024


def _cparams(sem=None):
    return pltpu.CompilerParams(dimension_semantics=sem, vmem_limit_bytes=VMEM_LIMIT)


def _cond_row(tile, rows_per_tile):
    n_prompt_tiles = N_P // rows_per_tile
    tiles_per_seq = DEC_SEQ // rows_per_tile
    return jnp.where(tile < n_prompt_tiles, 0, 1 + (tile - n_prompt_tiles) // tiles_per_seq)


def _mod_spec(layer, row_of_step):
    return pl.BlockSpec((None, None, 1, 6 * D), lambda b, *_: (layer, row_of_step(b), 0, 0))


def _full(shape):
    nd = len(shape)
    return pl.BlockSpec(shape, lambda *_: (0,) * nd)


def _rms_mod(x, g, sc, sh):
    ms = jnp.mean(x * x, axis=-1, keepdims=True)
    return (x * lax.rsqrt(ms + RMS_EPS) * g) * (1.0 + sc) + sh


def _ada_body(cond_ref, w_ref, b_ref, o_ref):
    c = cond_ref[...]
    a = (c * jax.nn.sigmoid(c)).astype(BF16)
    o_ref[0] = jnp.dot(a, w_ref[0].astype(BF16), preferred_element_type=F32) + b_ref[0]


def _ada_table(cond, w_ada, b_ada):
    tn = 1536
    return pl.pallas_call(
        _ada_body,
        grid=(DEPTH, 6 * D // tn),
        in_specs=[
            pl.BlockSpec((N_COND, D), lambda i, j: (0, 0)),
            pl.BlockSpec((1, D, tn), lambda i, j: (i, 0, j)),
            pl.BlockSpec((1, 1, tn), lambda i, j: (i, 0, j)),
        ],
        out_specs=pl.BlockSpec((1, N_COND, tn), lambda i, j: (i, 0, j)),
        out_shape=jax.ShapeDtypeStruct((DEPTH, N_COND, 6 * D), F32),
        compiler_params=_cparams(("arbitrary", "arbitrary")),
        name="ada_table",
    )(cond, w_ada, b_ada.reshape(DEPTH, 1, 6 * D))


def _half_norm(x, g, lo):
    ss = x * x
    s_lo = jnp.sum(jnp.where(lo, ss, 0.0), axis=-1, keepdims=True)
    s_hi = jnp.sum(jnp.where(lo, 0.0, ss), axis=-1, keepdims=True)
    ms = jnp.where(lo, s_lo, s_hi) * (1.0 / HEAD_DIM)
    return x * lax.rsqrt(ms + RMS_EPS) * g


def _rope(x, cos, sin_signed, first16):
    partner = jnp.where(first16, pltpu.roll(x, 128 - 16, axis=1), pltpu.roll(x, 16, axis=1))
    return x * cos + partner * sin_signed


def _dot_nt(a, b):
    return lax.dot_general(a, b, (((1,), (1,)), ((), ())), preferred_element_type=F32)


def _attn_body(*refs, seq, lam_init, latent, q_blk, kv_alias):
    it = iter(refs)
    x_ref, mod_ref, n1g_ref, wqkv_ref, wo_ref, qg_ref, kg_ref = (next(it) for _ in range(7))
    lq1_ref, lk1_ref, lq2_ref, lk2_ref, subg_ref = (next(it) for _ in range(5))
    if latent:
        cos_ref, sin_ref, ck_ref, cv_ref = (next(it) for _ in range(4))
    if kv_alias:
        next(it), next(it)
    xo_ref = next(it)
    if not latent:
        ko_ref, vo_ref = next(it), next(it)
    qkv_s, qlo_s, qhi_s, kb_s, vb_s, oh_s, o_s = (next(it) for _ in range(7))

    mod = mod_ref[...]
    sh1, sc1, g1 = mod[:, 0:D], mod[:, D:2 * D], mod[:, 2 * D:3 * D]
    n_col = (2 * QK_W + H * V_DIM) // 128
    for c in range(seq // TM):
        rs = slice(c * TM, (c + 1) * TM)
        h = _rms_mod(x_ref[rs, :], n1g_ref[...], sc1, sh1).astype(BF16)
        qkv = jnp.dot(h, wqkv_ref[...], preferred_element_type=F32)
        for cb in range(n_col):
            qkv_s[cb, rs, :] = qkv[:, cb * 128:(cb + 1) * 128]

    lam = (jnp.exp(jnp.sum(lq1_ref[...] * lk1_ref[...], axis=-1, keepdims=True))
           - jnp.exp(jnp.sum(lq2_ref[...] * lk2_ref[...], axis=-1, keepdims=True)) + lam_init)

    lane = lax.broadcasted_iota(jnp.int32, (1, 2 * HEAD_DIM), 1)
    lo = lane < HEAD_DIM
    first16 = (lane % 32) < 16
    scale = 1.0 / math.sqrt(HEAD_DIM)

    def head(hd, carry):
        qn = _half_norm(qkv_s[hd], qg_ref[...], lo)
        kn = _half_norm(qkv_s[H + hd], kg_ref[...], lo)
        vh = qkv_s[2 * H + hd]
        if latent:
            qn = _rope(qn, cos_ref[...], sin_ref[...], first16)
            kn = _rope(kn, cos_ref[...], sin_ref[...], first16)
        else:
            ko_ref[hd, 0] = kn[:, :HEAD_DIM]
            ko_ref[hd, 1] = kn[:, HEAD_DIM:]
            vo_ref[hd] = vh
        qs = qn * scale
        qlo_s[...] = jnp.where(lo, qs, 0.0).astype(BF16)
        qhi_s[...] = jnp.where(lo, 0.0, qs).astype(BF16)
        kb_s[0:seq, :] = kn.astype(BF16)
        vb_s[0:seq, :] = vh.astype(BF16)
        if latent:
            kb_s[seq:seq + PAST_LEN, :] = ck_ref[hd].astype(BF16)
            vb_s[seq:seq + PAST_LEN, :] = cv_ref[hd].astype(BF16)

        def q_block(qb, carry2):
            r0 = pl.multiple_of(qb * q_blk, q_blk)
            outs = []
            for q_s in (qlo_s, qhi_s):
                s = _dot_nt(q_s[pl.ds(r0, q_blk), :], kb_s[...])
                e = jnp.exp(s - jnp.max(s, axis=-1, keepdims=True))
                pv = jnp.dot(e.astype(BF16), vb_s[...], preferred_element_type=F32)
                outs.append(pv / jnp.sum(e, axis=-1, keepdims=True))
            o = outs[0] - lam * outs[1]
            on = o * lax.rsqrt(jnp.mean(o * o, axis=-1, keepdims=True) + RMS_EPS) * subg_ref[...] * (1.0 - lam_init)
            oh_s[hd, pl.ds(r0, q_blk), :] = on.astype(BF16)
            return carry2

        lax.fori_loop(0, seq // q_blk, q_block, 0)
        return carry

    lax.fori_loop(0, H, head, 0)

    for hd in range(H):
        o_s[:, hd * 128:(hd + 1) * 128] = oh_s[hd]
    for c in range(seq // TM):
        rs = slice(c * TM, (c + 1) * TM)
        out = jnp.dot(o_s[rs, :], wo_ref[...], preferred_element_type=F32)
        xo_ref[rs, :] = x_ref[rs, :] + g1 * out


def _attention(x_all, mod, layer, j, latent, p, rope_tabs, ck, cv, kv_acc=None):
    seq = DEC_SEQ if latent else SEQ
    nb = DEC_BATCH if latent else BATCH
    blk_off = N_P // seq if latent else 0
    n_keys = seq + PAST_LEN if latent else seq
    lam_init = 0.8 - 0.6 * math.exp(-0.3 * layer)
    row = (lambda b: 1 + b) if latent else (lambda b: 0)
    two = lambda v: jnp.concatenate([v, v])[None, :]
    vec = lambda v: v[None, :]

    args = [x_all, mod, vec(p["norm1_g"][layer]), p["attn_w_qkv"][j].astype(BF16), p["attn_w_o"][j].astype(BF16),
            two(p["attn_q_g"][j]), two(p["attn_k_g"][j]), vec(p["attn_lq1"][j]), vec(p["attn_lk1"][j]),
            vec(p["attn_lq2"][j]), vec(p["attn_lk2"][j]), vec(p["attn_sub_g"][j])]
    in_specs = [pl.BlockSpec((seq, D), lambda b: (blk_off + b, 0)), _mod_spec(layer, row), _full((1, D)),
                _full((D, 2 * QK_W + H * V_DIM)), _full((H * V_DIM, D)), _full((1, 128)), _full((1, 128)),
                _full((1, HEAD_DIM)), _full((1, HEAD_DIM)), _full((1, HEAD_DIM)), _full((1, HEAD_DIM)),
                _full((1, V_DIM))]
    out_shape = [jax.ShapeDtypeStruct((N_TOK, D), F32)]
    out_specs = [pl.BlockSpec((seq, D), lambda b: (blk_off + b, 0))]
    aliases = {0: 0}
    if latent:
        args += [rope_tabs[0], rope_tabs[1], ck, cv]
        in_specs += [_full((seq, 128)), _full((seq, 128)),
                     pl.BlockSpec((None, None, H, PAST_LEN, 128), lambda b: (b, j, 0, 0, 0)),
                     pl.BlockSpec((None, None, H, PAST_LEN, V_DIM), lambda b: (b, j, 0, 0, 0))]
    else:
        n_attn = (DEPTH + N_MIXERS - 1) // N_MIXERS
        out_shape += [jax.ShapeDtypeStruct((nb, n_attn, H, 2, seq, HEAD_DIM), F32),
                      jax.ShapeDtypeStruct((nb, n_attn, H, seq, V_DIM), F32)]
        out_specs += [pl.BlockSpec((None, None, H, 2, seq, HEAD_DIM), lambda b: (b, j, 0, 0, 0, 0)),
                      pl.BlockSpec((None, None, H, seq, V_DIM), lambda b: (b, j, 0, 0, 0))]
        if kv_acc is not None:
            aliases = {0: 0, len(args): 1, len(args) + 1: 2}
            args += list(kv_acc)
            in_specs += [pl.BlockSpec(memory_space=pl.ANY), pl.BlockSpec(memory_space=pl.ANY)]
    return pl.pallas_call(
        functools.partial(_attn_body, seq=seq, lam_init=lam_init, latent=latent, q_blk=256,
                          kv_alias=kv_acc is not None),
        grid=(nb,),
        in_specs=in_specs,
        out_specs=out_specs,
        out_shape=out_shape,
        scratch_shapes=[pltpu.VMEM(((2 * QK_W + H * V_DIM) // 128, seq, 128), F32),
                        pltpu.VMEM((seq, 128), BF16), pltpu.VMEM((seq, 128), BF16),
                        pltpu.VMEM((n_keys, 128), BF16), pltpu.VMEM((n_keys, V_DIM), BF16),
                        pltpu.VMEM((H, seq, V_DIM), BF16), pltpu.VMEM((seq, H * V_DIM), BF16)],
        input_output_aliases=aliases,
        compiler_params=_cparams(("arbitrary",)),
        name="attn_latent" if latent else "attn_prompt",
    )(*args)


def _conv_body(x_ref, mod_ref, n1g_ref, w1_ref, b1_ref, wdw_ref, bdw_ref, lng_ref, lnb_ref, w2_ref, b2_ref,
               xo_ref, upad, cv_s, *, seq):
    mod = mod_ref[...]
    sh1, sc1, g1 = mod[:, 0:D], mod[:, D:2 * D], mod[:, 2 * D:3 * D]
    zeros = jnp.zeros((HALO, D), F32)
    upad[0:HALO, :] = zeros
    upad[HALO + seq:2 * HALO + seq, :] = zeros
    for c in range(seq // TM):
        rs = slice(c * TM, (c + 1) * TM)
        h = _rms_mod(x_ref[rs, :], n1g_ref[...], sc1, sh1).astype(BF16)
        y = jnp.dot(h, w1_ref[...], preferred_element_type=F32) + b1_ref[...]
        upad[HALO + c * TM:HALO + (c + 1) * TM, :] = y[:, :D] * jax.nn.sigmoid(y[:, D:])

    pad = CONV_WIDTH // 2
    cb = 256

    def conv_rows(rb, carry):
        r0 = pl.multiple_of(rb * ROW_BLK, ROW_BLK)
        for c in range(D // cb):
            cs = slice(c * cb, (c + 1) * cb)
            win = upad[pl.ds(r0, ROW_BLK + 2 * HALO), cs]
            acc = jnp.zeros((ROW_BLK, cb), F32) + bdw_ref[:, cs]
            for k in range(CONV_WIDTH):
                off = HALO - pad + k
                acc = acc + win[off:off + ROW_BLK, :] * wdw_ref[k:k + 1, cs]
            cv_s[pl.ds(r0, ROW_BLK), cs] = acc
        return carry

    lax.fori_loop(0, seq // ROW_BLK, conv_rows, 0)

    for c in range(seq // TM):
        rs = slice(c * TM, (c + 1) * TM)
        v = cv_s[rs, :]
        mu = jnp.mean(v, axis=-1, keepdims=True)
        var = jnp.mean(jnp.square(v - mu), axis=-1, keepdims=True)
        yn = (v - mu) * lax.rsqrt(var + LN_EPS) * lng_ref[...] + lnb_ref[...]
        u2 = (yn * jax.nn.sigmoid(yn)).astype(BF16)
        out = jnp.dot(u2, w2_ref[...], preferred_element_type=F32) + b2_ref[...]
        xo_ref[rs, :] = x_ref[rs, :] + g1 * out


def _conformer(x_all, mod, layer, j, latent, p):
    seq = DEC_SEQ if latent else SEQ
    nb = DEC_BATCH if latent else BATCH
    blk_off = N_P // seq if latent else 0
    row = (lambda b: 1 + b) if latent else (lambda b: 0)
    vec = lambda v: v[None, :]
    return pl.pallas_call(
        functools.partial(_conv_body, seq=seq),
        grid=(nb,),
        in_specs=[pl.BlockSpec((seq, D), lambda b: (blk_off + b, 0)), _mod_spec(layer, row), _full((1, D)),
                  _full((D, 2 * D)), _full((1, 2 * D)), _full((CONV_WIDTH, D)), _full((1, D)), _full((1, D)),
                  _full((1, D)), _full((D, D)), _full((1, D))],
        out_specs=pl.BlockSpec((seq, D), lambda b: (blk_off + b, 0)),
        out_shape=jax.ShapeDtypeStruct((N_TOK, D), F32),
        scratch_shapes=[pltpu.VMEM((seq + 2 * HALO, D), F32), pltpu.VMEM((seq, D), F32)],
        input_output_aliases={0: 0},
        compiler_params=_cparams(("arbitrary",)),
        name="conv_latent" if latent else "conv_prompt",
    )(x_all, mod, vec(p["norm1_g"][layer]), p["conv_w_pw1"][j].astype(BF16), vec(p["conv_b_pw1"][j]),
      p["conv_w_dw"][j], vec(p["conv_b_dw"][j]), vec(p["conv_ln_g"][j]), vec(p["conv_ln_b"][j]),
      p["conv_w_pw2"][j].astype(BF16), vec(p["conv_b_pw2"][j]))


def _pool_body(x_ref, mod_ref, n1g_ref, wp_ref, ps_ref, xo_ref, hpad, p_s, *, seq):
    mod = mod_ref[...]
    sh1, sc1, g1 = mod[:, 0:D], mod[:, D:2 * D], mod[:, 2 * D:3 * D]
    zeros = jnp.zeros((HALO, D), F32)
    hpad[0:HALO, :] = zeros
    hpad[HALO + seq:2 * HALO + seq, :] = zeros
    for c in range(seq // TM):
        rs = slice(c * TM, (c + 1) * TM)
        hpad[HALO + c * TM:HALO + (c + 1) * TM, :] = _rms_mod(x_ref[rs, :], n1g_ref[...], sc1, sh1)

    def pool_rows(rb, carry):
        r0 = pl.multiple_of(rb * ROW_BLK, ROW_BLK)
        t = r0 + lax.broadcasted_iota(jnp.int32, (ROW_BLK, 1), 0)
        for g, w in enumerate(POOL_WINDOWS):
            cs = slice(g * POOL_GROUP, (g + 1) * POOL_GROUP)
            win = hpad[pl.ds(r0, ROW_BLK + 2 * HALO), cs]
            acc = jnp.zeros((ROW_BLK, POOL_GROUP), F32)
            for d in range(-(w // 2), w - w // 2):
                acc = acc + win[HALO + d:HALO + d + ROW_BLK, :]
            cnt = jnp.minimum(t + (w - w // 2), seq) - jnp.maximum(t - w // 2, 0)
            pooled = acc / cnt.astype(F32) - win[HALO:HALO + ROW_BLK, :]
            p_s[pl.ds(r0, ROW_BLK), cs] = pooled.astype(BF16)
        return carry

    lax.fori_loop(0, seq // ROW_BLK, pool_rows, 0)

    for c in range(seq // TM):
        rs = slice(c * TM, (c + 1) * TM)
        for g in range(len(POOL_WINDOWS)):
            cs = slice(g * POOL_GROUP, (g + 1) * POOL_GROUP)
            y = jnp.dot(p_s[rs, cs], wp_ref[g], preferred_element_type=F32) * ps_ref[:, cs]
            xo_ref[rs, cs] = x_ref[rs, cs] + g1[:, cs] * y


def _pool_mixer(x_all, mod, layer, j, latent, p):
    seq = DEC_SEQ if latent else SEQ
    nb = DEC_BATCH if latent else BATCH
    blk_off = N_P // seq if latent else 0
    row = (lambda b: 1 + b) if latent else (lambda b: 0)
    ng = len(POOL_WINDOWS)
    return pl.pallas_call(
        functools.partial(_pool_body, seq=seq),
        grid=(nb,),
        in_specs=[pl.BlockSpec((seq, D), lambda b: (blk_off + b, 0)), _mod_spec(layer, row), _full((1, D)),
                  _full((ng, POOL_GROUP, POOL_GROUP)), _full((1, D))],
        out_specs=pl.BlockSpec((seq, D), lambda b: (blk_off + b, 0)),
        out_shape=jax.ShapeDtypeStruct((N_TOK, D), F32),
        scratch_shapes=[pltpu.VMEM((seq + 2 * HALO, D), F32), pltpu.VMEM((seq, D), BF16)],
        input_output_aliases={0: 0},
        compiler_params=_cparams(("arbitrary",)),
        name="pool_latent" if latent else "pool_prompt",
    )(x_all, mod, p["norm1_g"][layer][None, :], p["pool_w"][j].astype(BF16), p["pool_scale"][j][None, :])


def _router_body(x_ref, mod_ref, n2g_ref, wr_ref, br_ref, h_ref, idx_ref, gate_ref, cnt_ref, cnt_s):
    t = pl.program_id(0)

    @pl.when(t == 0)
    def _():
        cnt_s[...] = jnp.zeros_like(cnt_s)

    mod = mod_ref[...]
    sh2, sc2 = mod[:, 3 * D:4 * D], mod[:, 4 * D:5 * D]
    h = _rms_mod(x_ref[...], n2g_ref[...], sc2, sh2)
    for s in range(D // 128):
        h_ref[pl.ds(s, TM, stride=8), :] = h[:, s * 128:(s + 1) * 128]
    logits = jnp.dot(h.astype(BF16), wr_ref[...], preferred_element_type=F32) + br_ref[...]
    lane = lax.broadcasted_iota(jnp.int32, logits.shape, 1)
    neg_inf = float("-inf")
    vals, idxs = [], []
    for _ in range(TOP_K):
        m = jnp.max(logits, axis=-1, keepdims=True)
        ix = jnp.min(jnp.where(logits == m, lane, 128), axis=-1, keepdims=True)
        vals.append(m)
        idxs.append(ix)
        logits = jnp.where(lane == ix, neg_inf, logits)
    es = [jnp.exp(v - vals[0]) for v in vals]
    den = es[0] + es[1] + es[2] + es[3]

    member = jnp.zeros(lane.shape, F32)
    for k in range(TOP_K):
        member = jnp.where(lane == idxs[k], 1.0, member)
    r_i = lax.broadcasted_iota(jnp.int32, (TM, TM), 0)
    c_i = lax.broadcasted_iota(jnp.int32, (TM, TM), 1)
    earlier = jnp.where(c_i < r_i, 1.0, 0.0).astype(BF16)
    before = jnp.dot(earlier, member.astype(BF16), preferred_element_type=F32) + cnt_s[...]
    cnt_s[...] = cnt_s[...] + jnp.sum(member, axis=0, keepdims=True)
    cnt_ref[...] = cnt_s[...].astype(jnp.int32)

    idx_out = jnp.zeros(lane.shape, jnp.int32)
    gate_out = jnp.zeros(lane.shape, F32)
    for k in range(TOP_K):
        rank_k = jnp.sum(jnp.where(lane == idxs[k], before, 0.0), axis=-1, keepdims=True).astype(jnp.int32)
        idx_out = jnp.where(lane == k, idxs[k], idx_out)
        idx_out = jnp.where(lane == TOP_K + k, rank_k, idx_out)
        gate_out = jnp.where(lane == k, es[k] / den, gate_out)
    idx_ref[...] = idx_out
    gate_ref[...] = gate_out


def _router(x_all, mod, layer, p):
    wr = jnp.zeros((D, 128), BF16).at[:, :E].set(p["moe_w_router"][layer].astype(BF16))
    br = jnp.full((1, 128), float("-inf"), F32).at[0, :E].set(p["moe_b_router"][layer])
    row = lambda t: _cond_row(t, TM)
    return pl.pallas_call(
        _router_body,
        grid=(N_TOK // TM,),
        in_specs=[pl.BlockSpec((TM, D), lambda t: (t, 0)), _mod_spec(layer, row), _full((1, D)),
                  _full((D, 128)), _full((1, 128))],
        out_specs=[pl.BlockSpec((TM * 8, 128), lambda t: (t, 0)), pl.BlockSpec((TM, 128), lambda t: (t, 0)),
                   pl.BlockSpec((TM, 128), lambda t: (t, 0)), pl.BlockSpec((1, 128), lambda t: (0, 0))],
        out_shape=[jax.ShapeDtypeStruct((N_TOK * 8, 128), F32), jax.ShapeDtypeStruct((N_TOK, 128), jnp.int32),
                   jax.ShapeDtypeStruct((N_TOK, 128), F32), jax.ShapeDtypeStruct((1, 128), jnp.int32)],
        scratch_shapes=[pltpu.VMEM((1, 128), F32)],
        compiler_params=_cparams(("arbitrary",)),
        name="router",
    )(x_all, mod, p["norm2_g"][layer][None, :], wr, br)


def _dispatch_plan(idx_rank, counts):
    counts = counts[0, :E]
    padded = ((counts + TM - 1) // TM) * TM
    pad_end = jnp.cumsum(padded)
    pad_start = pad_end - padded
    top_idx, rank = idx_rank[:, :TOP_K], idx_rank[:, TOP_K:2 * TOP_K]
    onehot = top_idx[:, :, None] == jnp.arange(E, dtype=jnp.int32)[None, None, :]
    dest = (jnp.sum(jnp.where(onehot, pad_start[None, None, :], 0), axis=-1) + rank).reshape(NK)
    as_i32 = lambda v: v.astype(jnp.int32)
    return as_i32(dest), as_i32(pad_start // TM), as_i32(padded // TM), as_i32(pad_start + counts), as_i32(pad_end)


def _tile_rows(row):
    return pl.ds(row * 8 if isinstance(row, int) else pl.multiple_of(row * 8, 8), 8)


def _dispatch_body(dest_ref, fill_lo_ref, fill_hi_ref, h_ref, xs_hbm, sem, fill_sem):
    i = pl.program_id(0)
    base = i * (TM * TOP_K)

    def token(t, carry):
        src = h_ref.at[_tile_rows(t)]
        for k in range(TOP_K):
            pltpu.make_async_copy(src, xs_hbm.at[_tile_rows(dest_ref[base + t * TOP_K + k])], sem).start(priority=k % 2)
        return carry

    lax.fori_loop(0, TM, token, 0, unroll=4)

    @pl.when(i == 0)
    def _():
        def fill_expert(e, carry):
            def start(slot, c2):
                pltpu.make_async_copy(h_ref.at[_tile_rows(0)], xs_hbm.at[_tile_rows(slot)], fill_sem).start()
                return c2

            def wait(slot, c2):
                pltpu.make_async_copy(h_ref.at[_tile_rows(0)], xs_hbm.at[_tile_rows(slot)], fill_sem).wait()
                return c2

            lax.fori_loop(fill_lo_ref[e], fill_hi_ref[e], start, 0)
            lax.fori_loop(fill_lo_ref[e], fill_hi_ref[e], wait, 0)
            return carry

        lax.fori_loop(0, E, fill_expert, 0)

    n = TM * TOP_K * 8
    pltpu.make_async_copy(xs_hbm.at[pl.ds(0, n)], xs_hbm.at[pl.ds(0, n)], sem).wait()


def _dispatch(dest, fill_lo, fill_hi, h_tok):
    return pl.pallas_call(
        _dispatch_body,
        grid_spec=pltpu.PrefetchScalarGridSpec(
            num_scalar_prefetch=3,
            grid=(N_TOK // TM,),
            in_specs=[pl.BlockSpec((TM * 8, 128), lambda i, *_: (i, 0))],
            out_specs=pl.BlockSpec(memory_space=pl.ANY),
            scratch_shapes=[pltpu.SemaphoreType.DMA, pltpu.SemaphoreType.DMA],
        ),
        out_shape=jax.ShapeDtypeStruct((R_SLOTS * 8, 128), F32),
        compiler_params=_cparams(("arbitrary",)),
        name="dispatch_rows",
    )(dest, fill_lo, fill_hi, h_tok)


def _expert_body(first_ref, ntiles_ref, xs_hbm, wgu_ref, bgu_ref, wdn_ref, bdn_ref, ys_hbm,
                 wgu_bf, wdn_bf, x_s, y_s, in_sem, out_sem):
    e = pl.program_id(0)
    t0 = first_ref[e]
    nt = ntiles_ref[e]

    def tile_rows(tile):
        return pl.ds(pl.multiple_of(tile * (TM * 8), TM * 8), TM * 8)

    def x_copy(tile, slot):
        return pltpu.make_async_copy(xs_hbm.at[tile_rows(tile)], x_s.at[slot], in_sem.at[slot])

    def y_copy(tile, slot):
        return pltpu.make_async_copy(y_s.at[slot], ys_hbm.at[tile_rows(tile)], out_sem.at[slot])

    @pl.when(nt > 0)
    def _():
        x_copy(t0, 0).start()

    for c in range(D // TM):
        rs = slice(c * TM, (c + 1) * TM)
        wgu_bf[rs, :] = wgu_ref[0, rs, :].astype(BF16)
        wdn_bf[rs, :] = wdn_ref[0, rs, :].astype(BF16)

    def tile_step(j, carry):
        slot = j % 2

        @pl.when(j + 1 < nt)
        def _():
            x_copy(t0 + j + 1, 1 - slot).start()

        x_copy(t0 + j, slot).wait()

        @pl.when(j >= 2)
        def _():
            y_copy(t0 + j - 2, slot).wait()

        x = jnp.concatenate([x_s[slot, pl.ds(s, TM, stride=8), :].astype(BF16) for s in range(D // 128)], axis=1)
        gu = jnp.dot(x, wgu_bf[...], preferred_element_type=F32) + bgu_ref[0]
        gate = jnp.minimum(gu[:, :FF], SWIGLU_LIMIT)
        up = jnp.clip(gu[:, FF:], -SWIGLU_LIMIT, SWIGLU_LIMIT)
        glu = gate * jax.nn.sigmoid(SWIGLU_ALPHA * gate)
        act = ((up + 1.0) * glu).astype(BF16)
        y = jnp.dot(act, wdn_bf[...], preferred_element_type=F32) + bdn_ref[0]
        for s in range(D // 128):
            y_s[slot, pl.ds(s, TM, stride=8), :] = y[:, s * 128:(s + 1) * 128]
        y_copy(t0 + j, slot).start()
        return carry

    lax.fori_loop(0, nt, tile_step, 0)

    @pl.when(nt >= 2)
    def _():
        y_copy(t0 + nt - 2, nt % 2).wait()

    @pl.when(nt >= 1)
    def _():
        y_copy(t0 + nt - 1, (nt - 1) % 2).wait()


def _experts(first_tile, n_tiles, xs, layer, p):
    return pl.pallas_call(
        _expert_body,
        grid_spec=pltpu.PrefetchScalarGridSpec(
            num_scalar_prefetch=2,
            grid=(E,),
            in_specs=[pl.BlockSpec(memory_space=pl.ANY),
                      pl.BlockSpec((None, 1, D, 2 * FF), lambda e, *_: (layer, e, 0, 0)),
                      pl.BlockSpec((None, 1, 1, 2 * FF), lambda e, *_: (layer, e, 0, 0)),
                      pl.BlockSpec((None, 1, FF, D), lambda e, *_: (layer, e, 0, 0)),
                      pl.BlockSpec((None, 1, 1, D), lambda e, *_: (layer, e, 0, 0))],
            out_specs=pl.BlockSpec(memory_space=pl.ANY),
            scratch_shapes=[pltpu.VMEM((D, 2 * FF), BF16), pltpu.VMEM((FF, D), BF16),
                            pltpu.VMEM((2, TM * 8, 128), F32), pltpu.VMEM((2, TM * 8, 128), F32),
                            pltpu.SemaphoreType.DMA((2,)), pltpu.SemaphoreType.DMA((2,))],
        ),
        out_shape=jax.ShapeDtypeStruct((R_SLOTS * 8, 128), F32),
        compiler_params=_cparams(("arbitrary",)),
        name="experts",
    )(first_tile, n_tiles, xs, p["moe_w_gate_up"], p["moe_b_gate_up"].reshape(DEPTH, E, 1, 2 * FF),
      p["moe_w_down"], p["moe_b_down"].reshape(DEPTH, E, 1, D))


def _combine_body(dest_ref, x_ref, mod_ref, gate_ref, ys_hbm, xo_ref, yg_s, sem):
    i = pl.program_id(0)

    def fetch(tile, slot):
        base = tile * (TM * TOP_K)

        def token(t, carry):
            for k in range(TOP_K):
                pltpu.make_async_copy(ys_hbm.at[_tile_rows(dest_ref[base + t * TOP_K + k])],
                                      yg_s.at[slot, k, _tile_rows(t)], sem.at[slot]).start(priority=k % 2)
            return carry

        lax.fori_loop(0, TM, token, 0, unroll=4)

    @pl.when(i == 0)
    def _():
        fetch(0, 0)

    @pl.when(i + 1 < pl.num_programs(0))
    def _():
        fetch(i + 1, (i + 1) % 2)

    slot = i % 2
    pltpu.make_async_copy(yg_s.at[slot], yg_s.at[slot], sem.at[slot]).wait()
    gates = gate_ref[...]
    for s in range(D // 128):
        cs = slice(s * 128, (s + 1) * 128)
        f = gates[:, 0:1] * yg_s[slot, 0, pl.ds(s, TM, stride=8), :]
        for k in range(1, TOP_K):
            f = f + gates[:, k:k + 1] * yg_s[slot, k, pl.ds(s, TM, stride=8), :]
        xo_ref[:, cs] = x_ref[:, cs] + mod_ref[:, 5 * D + s * 128:5 * D + (s + 1) * 128] * f


def _combine(x_all, mod, layer, dest, ys, gates):
    row = lambda t: _cond_row(t, TM)
    return pl.pallas_call(
        _combine_body,
        grid_spec=pltpu.PrefetchScalarGridSpec(
            num_scalar_prefetch=1,
            grid=(N_TOK // TM,),
            in_specs=[pl.BlockSpec((TM, D), lambda t, *_: (t, 0)), _mod_spec(layer, row),
                      pl.BlockSpec((TM, 128), lambda t, *_: (t, 0)), pl.BlockSpec(memory_space=pl.ANY)],
            out_specs=pl.BlockSpec((TM, D), lambda t, *_: (t, 0)),
            scratch_shapes=[pltpu.VMEM((2, TOP_K, TM * 8, 128), F32), pltpu.SemaphoreType.DMA((2,))],
        ),
        out_shape=jax.ShapeDtypeStruct((N_TOK, D), F32),
        input_output_aliases={1: 0},
        compiler_params=_cparams(("arbitrary",)),
        name="combine",
    )(dest, x_all, mod, gates, ys)


def _moe(x_all, mod, layer, p):
    h_tok, idx_rank, gates, counts = _router(x_all, mod, layer, p)
    dest, first_tile, n_tiles, fill_lo, fill_hi = _dispatch_plan(idx_rank, counts)
    xs = _dispatch(dest, fill_lo, fill_hi, h_tok)
    ys = _experts(first_tile, n_tiles, xs, layer, p)
    return _combine(x_all, mod, layer, dest, ys, gates)


def _rope_tables():
    rows = DEC_SEQ // GRID_W
    r = jnp.repeat(jnp.arange(rows), GRID_W).astype(F32)
    col = jnp.tile(jnp.arange(GRID_W), rows).astype(F32)
    half = HEAD_DIM // 2
    inv = ROPE_THETA ** (-jnp.arange(0, half, 2, dtype=F32) / half)
    ang_r = r[:, None] * inv[None, :]
    ang_c = col[:, None] * inv[None, :]
    cos64 = jnp.concatenate([jnp.cos(ang_r), jnp.cos(ang_r), jnp.cos(ang_c), jnp.cos(ang_c)], axis=-1)
    sin64 = jnp.concatenate([-jnp.sin(ang_r), jnp.sin(ang_r), -jnp.sin(ang_c), jnp.sin(ang_c)], axis=-1)
    return jnp.concatenate([cos64, cos64], axis=-1), jnp.concatenate([sin64, sin64], axis=-1)


def kernel(x_prompt, x_sample, cache_k, cache_v, c, c_ctx, norm1_g, norm2_g, w_ada, b_ada, attn_w_qkv, attn_w_o, attn_q_g, attn_k_g, attn_lq1, attn_lk1, attn_lq2, attn_lk2, attn_sub_g, conv_w_pw1, conv_b_pw1, conv_w_dw, conv_b_dw, conv_ln_g, conv_ln_b, conv_w_pw2, conv_b_pw2, pool_w, pool_scale, moe_w_router, moe_b_router, moe_w_gate_up, moe_b_gate_up, moe_w_down, moe_b_down):
    p = {
        "norm1_g": norm1_g, "norm2_g": norm2_g,
        "attn_w_qkv": attn_w_qkv, "attn_w_o": attn_w_o, "attn_q_g": attn_q_g, "attn_k_g": attn_k_g,
        "attn_lq1": attn_lq1, "attn_lk1": attn_lk1, "attn_lq2": attn_lq2, "attn_lk2": attn_lk2,
        "attn_sub_g": attn_sub_g,
        "conv_w_pw1": conv_w_pw1, "conv_b_pw1": conv_b_pw1, "conv_w_dw": conv_w_dw, "conv_b_dw": conv_b_dw,
        "conv_ln_g": conv_ln_g, "conv_ln_b": conv_ln_b, "conv_w_pw2": conv_w_pw2, "conv_b_pw2": conv_b_pw2,
        "pool_w": pool_w, "pool_scale": pool_scale,
        "moe_w_router": moe_w_router, "moe_b_router": moe_b_router, "moe_w_gate_up": moe_w_gate_up,
        "moe_b_gate_up": moe_b_gate_up, "moe_w_down": moe_w_down, "moe_b_down": moe_b_down,
    }
    cond = jnp.concatenate([c_ctx[None, :], c, jnp.zeros((N_COND - 1 - DEC_BATCH, D), F32)], axis=0)
    mod = _ada_table(cond, w_ada, b_ada).reshape(DEPTH, N_COND, 1, 6 * D)
    rope_tabs = _rope_tables()
    ck = cache_k.transpose(0, 1, 2, 4, 3, 5).reshape(DEC_BATCH, -1, H, PAST_LEN, 2 * HEAD_DIM)

    x_all = jnp.concatenate([x_prompt.reshape(N_P, D), x_sample.reshape(N_S, D)], axis=0)
    kv_acc = None
    for i in range(DEPTH):
        j, kind = i // N_MIXERS, i % N_MIXERS
        if kind == 0:
            x_all, *kv_acc = _attention(x_all, mod, i, j, False, p, None, None, None, kv_acc)
            (x_all,) = _attention(x_all, mod, i, j, True, p, rope_tabs, ck, cache_v)
        elif kind == 1:
            x_all = _conformer(x_all, mod, i, j, False, p)
            x_all = _conformer(x_all, mod, i, j, True, p)
        else:
            x_all = _pool_mixer(x_all, mod, i, j, False, p)
            x_all = _pool_mixer(x_all, mod, i, j, True, p)
        x_all = _moe(x_all, mod, i, p)
    y_prompt = x_all[:N_P].reshape(BATCH, SEQ, D)
    y_sample = x_all[N_P:].reshape(DEC_BATCH, DEC_SEQ, D)
    return (y_prompt, y_sample, kv_acc[0], kv_acc[1])
```

```python
import functools
import math

import jax
import jax.numpy as jnp
from jax import lax
from jax.experimental import pallas as pl
from jax.experimental.pallas import tpu as pltpu

F32 = jnp.float32
BF16 = jnp.bfloat16

D = 1024
DEPTH = 4
BATCH, SEQ = 32, 256
DEC_BATCH, DEC_SEQ = 4, 1024
PAST_LEN = 256
GRID_W = 64
N_MIXERS = 3
H = 8
HEAD_DIM = 64
V_DIM = 2 * HEAD_DIM
QK_W = H * 2 * HEAD_DIM
ROPE_THETA = 10000.0
CONV_WIDTH = 31
POOL_WINDOWS = (2, 4, 8, 16)
POOL_GROUP = D // 4
E = 32
TOP_K = 4
FF = D
SWIGLU_ALPHA = 1.702
SWIGLU_LIMIT = 7.0
RMS_EPS = 1e-6
LN_EPS = 1e-5

N_P = BATCH * SEQ
N_S = DEC_BATCH * DEC_SEQ
N_TOK = N_P + N_S
N_COND = 8

TM = 256
NK = N_TOK * TOP_K
N_TILES = NK // TM + E
R_SLOTS = N_TILES * TM
HALO = 16
ROW_BLK = 32
assert TOP_K == 4
VMEM_LIMIT = 56 * 1024 * 1024


def _cparams(sem=None):
    return pltpu.CompilerParams(dimension_semantics=sem, vmem_limit_bytes=VMEM_LIMIT)


def _cond_row(tile, rows_per_tile):
    n_prompt_tiles = N_P // rows_per_tile
    tiles_per_seq = DEC_SEQ // rows_per_tile
    return jnp.where(tile < n_prompt_tiles, 0, 1 + (tile - n_prompt_tiles) // tiles_per_seq)


def _mod_spec(layer, row_of_step):
    return pl.BlockSpec((None, None, 1, 6 * D), lambda b, *_: (layer, row_of_step(b), 0, 0))


def _full(shape):
    nd = len(shape)
    return pl.BlockSpec(shape, lambda *_: (0,) * nd)


def _rms_mod(x, g, sc, sh):
    ms = jnp.mean(x * x, axis=-1, keepdims=True)
    return (x * lax.rsqrt(ms + RMS_EPS) * g) * (1.0 + sc) + sh


def _ada_body(cond_ref, w_ref, b_ref, o_ref):
    c = cond_ref[...]
    a = (c * jax.nn.sigmoid(c)).astype(BF16)
    o_ref[0] = jnp.dot(a, w_ref[0].astype(BF16), preferred_element_type=F32) + b_ref[0]


def _ada_table(cond, w_ada, b_ada):
    tn = 1536
    return pl.pallas_call(
        _ada_body,
        grid=(DEPTH, 6 * D // tn),
        in_specs=[
            pl.BlockSpec((N_COND, D), lambda i, j: (0, 0)),
            pl.BlockSpec((1, D, tn), lambda i, j: (i, 0, j)),
            pl.BlockSpec((1, 1, tn), lambda i, j: (i, 0, j)),
        ],
        out_specs=pl.BlockSpec((1, N_COND, tn), lambda i, j: (i, 0, j)),
        out_shape=jax.ShapeDtypeStruct((DEPTH, N_COND, 6 * D), F32),
        compiler_params=_cparams(("arbitrary", "arbitrary")),
        name="ada_table",
    )(cond, w_ada, b_ada.reshape(DEPTH, 1, 6 * D))


def _half_norm(x, g, lo):
    ss = x * x
    s_lo = jnp.sum(jnp.where(lo, ss, 0.0), axis=-1, keepdims=True)
    s_hi = jnp.sum(jnp.where(lo, 0.0, ss), axis=-1, keepdims=True)
    ms = jnp.where(lo, s_lo, s_hi) * (1.0 / HEAD_DIM)
    return x * lax.rsqrt(ms + RMS_EPS) * g


def _rope(x, cos, sin_signed, first16):
    partner = jnp.where(first16, pltpu.roll(x, 128 - 16, axis=1), pltpu.roll(x, 16, axis=1))
    return x * cos + partner * sin_signed


def _dot_nt(a, b):
    return lax.dot_general(a, b, (((1,), (1,)), ((), ())), preferred_element_type=F32)


def _attn_body(*refs, seq, lam_init, latent, q_blk, kv_alias):
    it = iter(refs)
    x_ref, mod_ref, n1g_ref, wqkv_ref, wo_ref, qg_ref, kg_ref = (next(it) for _ in range(7))
    lq1_ref, lk1_ref, lq2_ref, lk2_ref, subg_ref = (next(it) for _ in range(5))
    if latent:
        cos_ref, sin_ref, ck_ref, cv_ref = (next(it) for _ in range(4))
    if kv_alias:
        next(it), next(it)
    xo_ref = next(it)
    if not latent:
        ko_ref, vo_ref = next(it), next(it)
    qkv_s, qlo_s, qhi_s, kb_s, vb_s, oh_s, o_s = (next(it) for _ in range(7))

    mod = mod_ref[...]
    sh1, sc1, g1 = mod[:, 0:D], mod[:, D:2 * D], mod[:, 2 * D:3 * D]
    n_col = (2 * QK_W + H * V_DIM) // 128
    for c in range(seq // TM):
        rs = slice(c * TM, (c + 1) * TM)
        h = _rms_mod(x_ref[rs, :], n1g_ref[...], sc1, sh1).astype(BF16)
        qkv = jnp.dot(h, wqkv_ref[...], preferred_element_type=F32)
        for cb in range(n_col):
            qkv_s[cb, rs, :] = qkv[:, cb * 128:(cb + 1) * 128]

    lam = (jnp.exp(jnp.sum(lq1_ref[...] * lk1_ref[...], axis=-1, keepdims=True))
           - jnp.exp(jnp.sum(lq2_ref[...] * lk2_ref[...], axis=-1, keepdims=True)) + lam_init)

    lane = lax.broadcasted_iota(jnp.int32, (1, 2 * HEAD_DIM), 1)
    lo = lane < HEAD_DIM
    first16 = (lane % 32) < 16
    scale = 1.0 / math.sqrt(HEAD_DIM)

    def head(hd, carry):
        qn = _half_norm(qkv_s[hd], qg_ref[...], lo)
        kn = _half_norm(qkv_s[H + hd], kg_ref[...], lo)
        vh = qkv_s[2 * H + hd]
        if latent:
            qn = _rope(qn, cos_ref[...], sin_ref[...], first16)
            kn = _rope(kn, cos_ref[...], sin_ref[...], first16)
        else:
            ko_ref[hd, 0] = kn[:, :HEAD_DIM]
            ko_ref[hd, 1] = kn[:, HEAD_DIM:]
            vo_ref[hd] = vh
        qs = qn * scale
        qlo_s[...] = jnp.where(lo, qs, 0.0).astype(BF16)
        qhi_s[...] = jnp.where(lo, 0.0, qs).astype(BF16)
        kb_s[0:seq, :] = kn.astype(BF16)
        vb_s[0:seq, :] = vh.astype(BF16)
        if latent:
            kb_s[seq:seq + PAST_LEN, :] = ck_ref[hd].astype(BF16)
            vb_s[seq:seq + PAST_LEN, :] = cv_ref[hd].astype(BF16)

        def q_block(qb, carry2):
            r0 = pl.multiple_of(qb * q_blk, q_blk)
            es, dens = [], []
            for q_s in (qlo_s, qhi_s):
                s = _dot_nt(q_s[pl.ds(r0, q_blk), :], kb_s[...])
                e = jnp.exp(s - jnp.max(s, axis=-1, keepdims=True))
                es.append(e)
                dens.append(jnp.sum(e, axis=-1, keepdims=True))
            a = (es[0] * (1.0 / dens[0]) - es[1] * (lam / dens[1])).astype(BF16)
            o = jnp.dot(a, vb_s[...], preferred_element_type=F32)
            on = o * lax.rsqrt(jnp.mean(o * o, axis=-1, keepdims=True) + RMS_EPS) * subg_ref[...] * (1.0 - lam_init)
            oh_s[hd, pl.ds(r0, q_blk), :] = on.astype(BF16)
            return carry2

        lax.fori_loop(0, seq // q_blk, q_block, 0)
        return carry

    lax.fori_loop(0, H, head, 0)

    for hd in range(H):
        o_s[:, hd * 128:(hd + 1) * 128] = oh_s[hd]
    for c in range(seq // TM):
        rs = slice(c * TM, (c + 1) * TM)
        out = jnp.dot(o_s[rs, :], wo_ref[...], preferred_element_type=F32)
        xo_ref[rs, :] = x_ref[rs, :] + g1 * out


def _attention(x_all, mod, layer, j, latent, p, rope_tabs, ck, cv, kv_acc=None):
    seq = DEC_SEQ if latent else SEQ
    nb = DEC_BATCH if latent else BATCH
    blk_off = N_P // seq if latent else 0
    n_keys = seq + PAST_LEN if latent else seq
    lam_init = 0.8 - 0.6 * math.exp(-0.3 * layer)
    row = (lambda b: 1 + b) if latent else (lambda b: 0)
    two = lambda v: jnp.concatenate([v, v])[None, :]
    vec = lambda v: v[None, :]

    args = [x_all, mod, vec(p["norm1_g"][layer]), p["attn_w_qkv"][j].astype(BF16), p["attn_w_o"][j].astype(BF16),
            two(p["attn_q_g"][j]), two(p["attn_k_g"][j]), vec(p["attn_lq1"][j]), vec(p["attn_lk1"][j]),
            vec(p["attn_lq2"][j]), vec(p["attn_lk2"][j]), vec(p["attn_sub_g"][j])]
    in_specs = [pl.BlockSpec((seq, D), lambda b: (blk_off + b, 0)), _mod_spec(layer, row), _full((1, D)),
                _full((D, 2 * QK_W + H * V_DIM)), _full((H * V_DIM, D)), _full((1, 128)), _full((1, 128)),
                _full((1, HEAD_DIM)), _full((1, HEAD_DIM)), _full((1, HEAD_DIM)), _full((1, HEAD_DIM)),
                _full((1, V_DIM))]
    out_shape = [jax.ShapeDtypeStruct((N_TOK, D), F32)]
    out_specs = [pl.BlockSpec((seq, D), lambda b: (blk_off + b, 0))]
    aliases = {0: 0}
    if latent:
        args += [rope_tabs[0], rope_tabs[1], ck, cv]
        in_specs += [_full((seq, 128)), _full((seq, 128)),
                     pl.BlockSpec((None, None, H, PAST_LEN, 128), lambda b: (b, j, 0, 0, 0)),
                     pl.BlockSpec((None, None, H, PAST_LEN, V_DIM), lambda b: (b, j, 0, 0, 0))]
    else:
        n_attn = (DEPTH + N_MIXERS - 1) // N_MIXERS
        out_shape += [jax.ShapeDtypeStruct((nb, n_attn, H, 2, seq, HEAD_DIM), F32),
                      jax.ShapeDtypeStruct((nb, n_attn, H, seq, V_DIM), F32)]
        out_specs += [pl.BlockSpec((None, None, H, 2, seq, HEAD_DIM), lambda b: (b, j, 0, 0, 0, 0)),
                      pl.BlockSpec((None, None, H, seq, V_DIM), lambda b: (b, j, 0, 0, 0))]
        if kv_acc is not None:
            aliases = {0: 0, len(args): 1, len(args) + 1: 2}
            args += list(kv_acc)
            in_specs += [pl.BlockSpec(memory_space=pl.ANY), pl.BlockSpec(memory_space=pl.ANY)]
    return pl.pallas_call(
        functools.partial(_attn_body, seq=seq, lam_init=lam_init, latent=latent, q_blk=256,
                          kv_alias=kv_acc is not None),
        grid=(nb,),
        in_specs=in_specs,
        out_specs=out_specs,
        out_shape=out_shape,
        scratch_shapes=[pltpu.VMEM(((2 * QK_W + H * V_DIM) // 128, seq, 128), F32),
                        pltpu.VMEM((seq, 128), BF16), pltpu.VMEM((seq, 128), BF16),
                        pltpu.VMEM((n_keys, 128), BF16), pltpu.VMEM((n_keys, V_DIM), BF16),
                        pltpu.VMEM((H, seq, V_DIM), BF16), pltpu.VMEM((seq, H * V_DIM), BF16)],
        input_output_aliases=aliases,
        compiler_params=_cparams(("arbitrary",)),
        name="attn_latent" if latent else "attn_prompt",
    )(*args)


def _conv_body(x_ref, mod_ref, n1g_ref, w1_ref, b1_ref, wdw_ref, bdw_ref, lng_ref, lnb_ref, w2_ref, b2_ref,
               xo_ref, upad, cv_s, *, seq):
    mod = mod_ref[...]
    sh1, sc1, g1 = mod[:, 0:D], mod[:, D:2 * D], mod[:, 2 * D:3 * D]
    zeros = jnp.zeros((HALO, D), F32)
    upad[0:HALO, :] = zeros
    upad[HALO + seq:2 * HALO + seq, :] = zeros
    for c in range(seq // TM):
        rs = slice(c * TM, (c + 1) * TM)
        h = _rms_mod(x_ref[rs, :], n1g_ref[...], sc1, sh1).astype(BF16)
        y = jnp.dot(h, w1_ref[...], preferred_element_type=F32) + b1_ref[...]
        upad[HALO + c * TM:HALO + (c + 1) * TM, :] = y[:, :D] * jax.nn.sigmoid(y[:, D:])

    pad = CONV_WIDTH // 2
    cb = 256

    def conv_rows(rb, carry):
        r0 = pl.multiple_of(rb * ROW_BLK, ROW_BLK)
        for c in range(D // cb):
            cs = slice(c * cb, (c + 1) * cb)
            win = upad[pl.ds(r0, ROW_BLK + 2 * HALO), cs]
            acc = jnp.zeros((ROW_BLK, cb), F32) + bdw_ref[:, cs]
            for k in range(CONV_WIDTH):
                off = HALO - pad + k
                acc = acc + win[off:off + ROW_BLK, :] * wdw_ref[k:k + 1, cs]
            cv_s[pl.ds(r0, ROW_BLK), cs] = acc
        return carry

    lax.fori_loop(0, seq // ROW_BLK, conv_rows, 0)

    for c in range(seq // TM):
        rs = slice(c * TM, (c + 1) * TM)
        v = cv_s[rs, :]
        mu = jnp.mean(v, axis=-1, keepdims=True)
        var = jnp.mean(jnp.square(v - mu), axis=-1, keepdims=True)
        yn = (v - mu) * lax.rsqrt(var + LN_EPS) * lng_ref[...] + lnb_ref[...]
        u2 = (yn * jax.nn.sigmoid(yn)).astype(BF16)
        out = jnp.dot(u2, w2_ref[...], preferred_element_type=F32) + b2_ref[...]
        xo_ref[rs, :] = x_ref[rs, :] + g1 * out


def _conformer(x_all, mod, layer, j, latent, p):
    seq = DEC_SEQ if latent else SEQ
    nb = DEC_BATCH if latent else BATCH
    blk_off = N_P // seq if latent else 0
    row = (lambda b: 1 + b) if latent else (lambda b: 0)
    vec = lambda v: v[None, :]
    return pl.pallas_call(
        functools.partial(_conv_body, seq=seq),
        grid=(nb,),
        in_specs=[pl.BlockSpec((seq, D), lambda b: (blk_off + b, 0)), _mod_spec(layer, row), _full((1, D)),
                  _full((D, 2 * D)), _full((1, 2 * D)), _full((CONV_WIDTH, D)), _full((1, D)), _full((1, D)),
                  _full((1, D)), _full((D, D)), _full((1, D))],
        out_specs=pl.BlockSpec((seq, D), lambda b: (blk_off + b, 0)),
        out_shape=jax.ShapeDtypeStruct((N_TOK, D), F32),
        scratch_shapes=[pltpu.VMEM((seq + 2 * HALO, D), F32), pltpu.VMEM((seq, D), F32)],
        input_output_aliases={0: 0},
        compiler_params=_cparams(("arbitrary",)),
        name="conv_latent" if latent else "conv_prompt",
    )(x_all, mod, vec(p["norm1_g"][layer]), p["conv_w_pw1"][j].astype(BF16), vec(p["conv_b_pw1"][j]),
      p["conv_w_dw"][j], vec(p["conv_b_dw"][j]), vec(p["conv_ln_g"][j]), vec(p["conv_ln_b"][j]),
      p["conv_w_pw2"][j].astype(BF16), vec(p["conv_b_pw2"][j]))


def _pool_body(x_ref, mod_ref, n1g_ref, wp_ref, ps_ref, xo_ref, hpad, p_s, *, seq):
    mod = mod_ref[...]
    sh1, sc1, g1 = mod[:, 0:D], mod[:, D:2 * D], mod[:, 2 * D:3 * D]
    zeros = jnp.zeros((HALO, D), F32)
    hpad[0:HALO, :] = zeros
    hpad[HALO + seq:2 * HALO + seq, :] = zeros
    for c in range(seq // TM):
        rs = slice(c * TM, (c + 1) * TM)
        hpad[HALO + c * TM:HALO + (c + 1) * TM, :] = _rms_mod(x_ref[rs, :], n1g_ref[...], sc1, sh1)

    def pool_rows(rb, carry):
        r0 = pl.multiple_of(rb * ROW_BLK, ROW_BLK)
        t = r0 + lax.broadcasted_iota(jnp.int32, (ROW_BLK, 1), 0)
        for g, w in enumerate(POOL_WINDOWS):
            cs = slice(g * POOL_GROUP, (g + 1) * POOL_GROUP)
            win = hpad[pl.ds(r0, ROW_BLK + 2 * HALO), cs]
            acc = jnp.zeros((ROW_BLK, POOL_GROUP), F32)
            for d in range(-(w // 2), w - w // 2):
                acc = acc + win[HALO + d:HALO + d + ROW_BLK, :]
            cnt = jnp.minimum(t + (w - w // 2), seq) - jnp.maximum(t - w // 2, 0)
            pooled = acc / cnt.astype(F32) - win[HALO:HALO + ROW_BLK, :]
            p_s[pl.ds(r0, ROW_BLK), cs] = pooled.astype(BF16)
        return carry

    lax.fori_loop(0, seq // ROW_BLK, pool_rows, 0)

    for c in range(seq // TM):
        rs = slice(c * TM, (c + 1) * TM)
        for g in range(len(POOL_WINDOWS)):
            cs = slice(g * POOL_GROUP, (g + 1) * POOL_GROUP)
            y = jnp.dot(p_s[rs, cs], wp_ref[g], preferred_element_type=F32) * ps_ref[:, cs]
            xo_ref[rs, cs] = x_ref[rs, cs] + g1[:, cs] * y


def _pool_mixer(x_all, mod, layer, j, latent, p):
    seq = DEC_SEQ if latent else SEQ
    nb = DEC_BATCH if latent else BATCH
    blk_off = N_P // seq if latent else 0
    row = (lambda b: 1 + b) if latent else (lambda b: 0)
    ng = len(POOL_WINDOWS)
    return pl.pallas_call(
        functools.partial(_pool_body, seq=seq),
        grid=(nb,),
        in_specs=[pl.BlockSpec((seq, D), lambda b: (blk_off + b, 0)), _mod_spec(layer, row), _full((1, D)),
                  _full((ng, POOL_GROUP, POOL_GROUP)), _full((1, D))],
        out_specs=pl.BlockSpec((seq, D), lambda b: (blk_off + b, 0)),
        out_shape=jax.ShapeDtypeStruct((N_TOK, D), F32),
        scratch_shapes=[pltpu.VMEM((seq + 2 * HALO, D), F32), pltpu.VMEM((seq, D), BF16)],
        input_output_aliases={0: 0},
        compiler_params=_cparams(("arbitrary",)),
        name="pool_latent" if latent else "pool_prompt",
    )(x_all, mod, p["norm1_g"][layer][None, :], p["pool_w"][j].astype(BF16), p["pool_scale"][j][None, :])


def _router_body(x_ref, mod_ref, n2g_ref, wr_ref, br_ref, h_ref, idx_ref, gate_ref, cnt_ref, cnt_s):
    t = pl.program_id(0)

    @pl.when(t == 0)
    def _():
        cnt_s[...] = jnp.zeros_like(cnt_s)

    mod = mod_ref[...]
    sh2, sc2 = mod[:, 3 * D:4 * D], mod[:, 4 * D:5 * D]
    h = _rms_mod(x_ref[...], n2g_ref[...], sc2, sh2)
    for s in range(D // 128):
        h_ref[pl.ds(s, TM, stride=8), :] = h[:, s * 128:(s + 1) * 128]
    logits = jnp.dot(h.astype(BF16), wr_ref[...], preferred_element_type=F32) + br_ref[...]
    lane = lax.broadcasted_iota(jnp.int32, logits.shape, 1)
    neg_inf = float("-inf")
    vals, idxs = [], []
    for _ in range(TOP_K):
        m = jnp.max(logits, axis=-1, keepdims=True)
        ix = jnp.min(jnp.where(logits == m, lane, 128), axis=-1, keepdims=True)
        vals.append(m)
        idxs.append(ix)
        logits = jnp.where(lane == ix, neg_inf, logits)
    es = [jnp.exp(v - vals[0]) for v in vals]
    den = es[0] + es[1] + es[2] + es[3]

    member = jnp.zeros(lane.shape, F32)
    for k in range(TOP_K):
        member = jnp.where(lane == idxs[k], 1.0, member)
    r_i = lax.broadcasted_iota(jnp.int32, (TM, TM), 0)
    c_i = lax.broadcasted_iota(jnp.int32, (TM, TM), 1)
    earlier = jnp.where(c_i < r_i, 1.0, 0.0).astype(BF16)
    before = jnp.dot(earlier, member.astype(BF16), preferred_element_type=F32) + cnt_s[...]
    cnt_s[...] = cnt_s[...] + jnp.sum(member, axis=0, keepdims=True)
    cnt_ref[...] = cnt_s[...].astype(jnp.int32)

    idx_out = jnp.zeros(lane.shape, jnp.int32)
    gate_out = jnp.zeros(lane.shape, F32)
    for k in range(TOP_K):
        rank_k = jnp.sum(jnp.where(lane == idxs[k], before, 0.0), axis=-1, keepdims=True).astype(jnp.int32)
        idx_out = jnp.where(lane == k, idxs[k], idx_out)
        idx_out = jnp.where(lane == TOP_K + k, rank_k, idx_out)
        gate_out = jnp.where(lane == k, es[k] / den, gate_out)
    idx_ref[...] = idx_out
    gate_ref[...] = gate_out


def _router(x_all, mod, layer, p):
    wr = jnp.zeros((D, 128), BF16).at[:, :E].set(p["moe_w_router"][layer].astype(BF16))
    br = jnp.full((1, 128), float("-inf"), F32).at[0, :E].set(p["moe_b_router"][layer])
    row = lambda t: _cond_row(t, TM)
    return pl.pallas_call(
        _router_body,
        grid=(N_TOK // TM,),
        in_specs=[pl.BlockSpec((TM, D), lambda t: (t, 0)), _mod_spec(layer, row), _full((1, D)),
                  _full((D, 128)), _full((1, 128))],
        out_specs=[pl.BlockSpec((TM * 8, 128), lambda t: (t, 0)), pl.BlockSpec((TM, 128), lambda t: (t, 0)),
                   pl.BlockSpec((TM, 128), lambda t: (t, 0)), pl.BlockSpec((1, 128), lambda t: (0, 0))],
        out_shape=[jax.ShapeDtypeStruct((N_TOK * 8, 128), F32), jax.ShapeDtypeStruct((N_TOK, 128), jnp.int32),
                   jax.ShapeDtypeStruct((N_TOK, 128), F32), jax.ShapeDtypeStruct((1, 128), jnp.int32)],
        scratch_shapes=[pltpu.VMEM((1, 128), F32)],
        compiler_params=_cparams(("arbitrary",)),
        name="router",
    )(x_all, mod, p["norm2_g"][layer][None, :], wr, br)


def _dispatch_plan(idx_rank, counts):
    counts = counts[0, :E]
    padded = ((counts + TM - 1) // TM) * TM
    pad_end = jnp.cumsum(padded)
    pad_start = pad_end - padded
    top_idx, rank = idx_rank[:, :TOP_K], idx_rank[:, TOP_K:2 * TOP_K]
    onehot = top_idx[:, :, None] == jnp.arange(E, dtype=jnp.int32)[None, None, :]
    dest = (jnp.sum(jnp.where(onehot, pad_start[None, None, :], 0), axis=-1) + rank).reshape(NK)
    as_i32 = lambda v: v.astype(jnp.int32)
    return as_i32(dest), as_i32(pad_start // TM), as_i32(padded // TM), as_i32(pad_start + counts), as_i32(pad_end)


def _tile_rows(row):
    return pl.ds(row * 8 if isinstance(row, int) else pl.multiple_of(row * 8, 8), 8)


def _dispatch_body(dest_ref, fill_lo_ref, fill_hi_ref, h_ref, xs_hbm, sem, fill_sem):
    i = pl.program_id(0)
    base = i * (TM * TOP_K)

    def token(t, carry):
        src = h_ref.at[_tile_rows(t)]
        for k in range(TOP_K):
            pltpu.make_async_copy(src, xs_hbm.at[_tile_rows(dest_ref[base + t * TOP_K + k])], sem).start(priority=k % 2)
        return carry

    lax.fori_loop(0, TM, token, 0, unroll=4)

    @pl.when(i == 0)
    def _():
        def fill_expert(e, carry):
            def start(slot, c2):
                pltpu.make_async_copy(h_ref.at[_tile_rows(0)], xs_hbm.at[_tile_rows(slot)], fill_sem).start()
                return c2

            def wait(slot, c2):
                pltpu.make_async_copy(h_ref.at[_tile_rows(0)], xs_hbm.at[_tile_rows(slot)], fill_sem).wait()
                return c2

            lax.fori_loop(fill_lo_ref[e], fill_hi_ref[e], start, 0)
            lax.fori_loop(fill_lo_ref[e], fill_hi_ref[e], wait, 0)
            return carry

        lax.fori_loop(0, E, fill_expert, 0)

    n = TM * TOP_K * 8
    pltpu.make_async_copy(xs_hbm.at[pl.ds(0, n)], xs_hbm.at[pl.ds(0, n)], sem).wait()


def _dispatch(dest, fill_lo, fill_hi, h_tok):
    return pl.pallas_call(
        _dispatch_body,
        grid_spec=pltpu.PrefetchScalarGridSpec(
            num_scalar_prefetch=3,
            grid=(N_TOK // TM,),
            in_specs=[pl.BlockSpec((TM * 8, 128), lambda i, *_: (i, 0))],
            out_specs=pl.BlockSpec(memory_space=pl.ANY),
            scratch_shapes=[pltpu.SemaphoreType.DMA, pltpu.SemaphoreType.DMA],
        ),
        out_shape=jax.ShapeDtypeStruct((R_SLOTS * 8, 128), F32),
        compiler_params=_cparams(("arbitrary",)),
        name="dispatch_rows",
    )(dest, fill_lo, fill_hi, h_tok)


W_CHUNKS = 8


def _expert_body(first_ref, ntiles_ref, xs_hbm, wgu_hbm, bgu_ref, wdn_hbm, bdn_ref, ys_hbm,
                 wgu_f, wdn_f, wgu_bf, wdn_bf, x_s, y_s, w_sem, in_sem, out_sem, *, layer):
    e = pl.program_id(0)
    w_slot = e % 2
    t0 = first_ref[e]
    nt = ntiles_ref[e]
    n_total = first_ref[E - 1] + ntiles_ref[E - 1]

    def weight_copies(expert, slot):
        gu_rows, dn_rows = D // W_CHUNKS, 2 * FF // W_CHUNKS
        cps = [pltpu.make_async_copy(wgu_hbm.at[layer, expert, pl.ds(c * gu_rows, gu_rows), :],
                                     wgu_f.at[slot, pl.ds(c * gu_rows, gu_rows), :], w_sem.at[slot])
               for c in range(W_CHUNKS)]
        cps += [pltpu.make_async_copy(wdn_hbm.at[layer, expert, pl.ds(c * dn_rows, dn_rows), :],
                                      wdn_f.at[slot, pl.ds(c * dn_rows, dn_rows), :], w_sem.at[slot])
                for c in range(FF // dn_rows)]
        return cps

    def start_weights(expert, slot):
        for n, cp in enumerate(weight_copies(expert, slot)):
            cp.start(priority=n % 2)

    half = TM * 8 // 2

    def x_copies(tile, slot):
        return [pltpu.make_async_copy(xs_hbm.at[pl.ds(pl.multiple_of(tile * (TM * 8) + h * half, half), half)],
                                      x_s.at[slot, pl.ds(h * half, half)], in_sem.at[slot]) for h in range(2)]

    def y_copies(tile, slot):
        return [pltpu.make_async_copy(y_s.at[slot, pl.ds(h * half, half)],
                                      ys_hbm.at[pl.ds(pl.multiple_of(tile * (TM * 8) + h * half, half), half)],
                                      out_sem.at[slot]) for h in range(2)]

    def start_all(copies):
        for n, cp in enumerate(copies):
            cp.start(priority=n % 2)

    def wait_all(copies):
        for cp in copies:
            cp.wait()

    @pl.when(e == 0)
    def _():
        start_weights(0, 0)

    @pl.when(e + 1 < E)
    def _():
        start_weights(e + 1, 1 - w_slot)

    @pl.when(jnp.logical_and(nt > 0, t0 == 0))
    def _():
        start_all(x_copies(0, 0))

    wait_all(weight_copies(e, w_slot))
    for c in range(D // TM):
        rs = slice(c * TM, (c + 1) * TM)
        wgu_bf[rs, :] = wgu_f[w_slot, rs, :].astype(BF16)
        wdn_bf[rs, :] = wdn_f[w_slot, rs, :].astype(BF16)

    def tile_step(j, carry):
        g = t0 + j
        slot = g % 2

        @pl.when(g + 1 < n_total)
        def _():
            start_all(x_copies(g + 1, 1 - slot))

        wait_all(x_copies(g, slot))

        @pl.when(g >= 2)
        def _():
            wait_all(y_copies(g - 2, slot))

        x = jnp.concatenate([x_s[slot, pl.ds(s, TM, stride=8), :].astype(BF16) for s in range(D // 128)], axis=1)
        gu = jnp.dot(x, wgu_bf[...], preferred_element_type=F32) + bgu_ref[0]
        gate = jnp.minimum(gu[:, :FF], SWIGLU_LIMIT)
        up = jnp.clip(gu[:, FF:], -SWIGLU_LIMIT, SWIGLU_LIMIT)
        glu = gate * jax.nn.sigmoid(SWIGLU_ALPHA * gate)
        act = ((up + 1.0) * glu).astype(BF16)
        y = jnp.dot(act, wdn_bf[...], preferred_element_type=F32) + bdn_ref[0]
        for s in range(D // 128):
            y_s[slot, pl.ds(s, TM, stride=8), :] = y[:, s * 128:(s + 1) * 128]
        start_all(y_copies(g, slot))
        return carry

    lax.fori_loop(0, nt, tile_step, 0)

    @pl.when(e == E - 1)
    def _():
        @pl.when(n_total >= 2)
        def _():
            wait_all(y_copies(n_total - 2, n_total % 2))

        @pl.when(n_total >= 1)
        def _():
            wait_all(y_copies(n_total - 1, (n_total - 1) % 2))


def _experts(first_tile, n_tiles, xs, layer, p):
    return pl.pallas_call(
        functools.partial(_expert_body, layer=layer),
        grid_spec=pltpu.PrefetchScalarGridSpec(
            num_scalar_prefetch=2,
            grid=(E,),
            in_specs=[pl.BlockSpec(memory_space=pl.ANY),
                      pl.BlockSpec(memory_space=pl.ANY),
                      pl.BlockSpec((None, 1, 1, 2 * FF), lambda e, *_: (layer, e, 0, 0)),
                      pl.BlockSpec(memory_space=pl.ANY),
                      pl.BlockSpec((None, 1, 1, D), lambda e, *_: (layer, e, 0, 0))],
            out_specs=pl.BlockSpec(memory_space=pl.ANY),
            scratch_shapes=[pltpu.VMEM((2, D, 2 * FF), F32), pltpu.VMEM((2, FF, D), F32),
                            pltpu.VMEM((D, 2 * FF), BF16), pltpu.VMEM((FF, D), BF16),
                            pltpu.VMEM((2, TM * 8, 128), F32), pltpu.VMEM((2, TM * 8, 128), F32),
                            pltpu.SemaphoreType.DMA((2,)), pltpu.SemaphoreType.DMA((2,)),
                            pltpu.SemaphoreType.DMA((2,))],
        ),
        out_shape=jax.ShapeDtypeStruct((R_SLOTS * 8, 128), F32),
        compiler_params=_cparams(("arbitrary",)),
        name="experts",
    )(first_tile, n_tiles, xs, p["moe_w_gate_up"], p["moe_b_gate_up"].reshape(DEPTH, E, 1, 2 * FF),
      p["moe_w_down"], p["moe_b_down"].reshape(DEPTH, E, 1, D))


def _combine_body(dest_ref, x_ref, mod_ref, gate_ref, ys_hbm, xo_ref, yg_s, sem):
    i = pl.program_id(0)

    def fetch(tile, slot):
        base = tile * (TM * TOP_K)

        def token(t, carry):
            for k in range(TOP_K):
                pltpu.make_async_copy(ys_hbm.at[_tile_rows(dest_ref[base + t * TOP_K + k])],
                                      yg_s.at[slot, k, _tile_rows(t)], sem.at[slot]).start(priority=k % 2)
            return carry

        lax.fori_loop(0, TM, token, 0, unroll=4)

    @pl.when(i == 0)
    def _():
        fetch(0, 0)

    @pl.when(i + 1 < pl.num_programs(0))
    def _():
        fetch(i + 1, (i + 1) % 2)

    slot = i % 2
    pltpu.make_async_copy(yg_s.at[slot], yg_s.at[slot], sem.at[slot]).wait()
    gates = gate_ref[...]
    for s in range(D // 128):
        cs = slice(s * 128, (s + 1) * 128)
        f = gates[:, 0:1] * yg_s[slot, 0, pl.ds(s, TM, stride=8), :]
        for k in range(1, TOP_K):
            f = f + gates[:, k:k + 1] * yg_s[slot, k, pl.ds(s, TM, stride=8), :]
        xo_ref[:, cs] = x_ref[:, cs] + mod_ref[:, 5 * D + s * 128:5 * D + (s + 1) * 128] * f


def _combine(x_all, mod, layer, dest, ys, gates):
    row = lambda t: _cond_row(t, TM)
    return pl.pallas_call(
        _combine_body,
        grid_spec=pltpu.PrefetchScalarGridSpec(
            num_scalar_prefetch=1,
            grid=(N_TOK // TM,),
            in_specs=[pl.BlockSpec((TM, D), lambda t, *_: (t, 0)), _mod_spec(layer, row),
                      pl.BlockSpec((TM, 128), lambda t, *_: (t, 0)), pl.BlockSpec(memory_space=pl.ANY)],
            out_specs=pl.BlockSpec((TM, D), lambda t, *_: (t, 0)),
            scratch_shapes=[pltpu.VMEM((2, TOP_K, TM * 8, 128), F32), pltpu.SemaphoreType.DMA((2,))],
        ),
        out_shape=jax.ShapeDtypeStruct((N_TOK, D), F32),
        input_output_aliases={1: 0},
        compiler_params=_cparams(("arbitrary",)),
        name="combine",
    )(dest, x_all, mod, gates, ys)


def _moe(x_all, mod, layer, p):
    h_tok, idx_rank, gates, counts = _router(x_all, mod, layer, p)
    dest, first_tile, n_tiles, fill_lo, fill_hi = _dispatch_plan(idx_rank, counts)
    xs = _dispatch(dest, fill_lo, fill_hi, h_tok)
    ys = _experts(first_tile, n_tiles, xs, layer, p)
    return _combine(x_all, mod, layer, dest, ys, gates)


def _rope_tables():
    rows = DEC_SEQ // GRID_W
    r = jnp.repeat(jnp.arange(rows), GRID_W).astype(F32)
    col = jnp.tile(jnp.arange(GRID_W), rows).astype(F32)
    half = HEAD_DIM // 2
    inv = ROPE_THETA ** (-jnp.arange(0, half, 2, dtype=F32) / half)
    ang_r = r[:, None] * inv[None, :]
    ang_c = col[:, None] * inv[None, :]
    cos64 = jnp.concatenate([jnp.cos(ang_r), jnp.cos(ang_r), jnp.cos(ang_c), jnp.cos(ang_c)], axis=-1)
    sin64 = jnp.concatenate([-jnp.sin(ang_r), jnp.sin(ang_r), -jnp.sin(ang_c), jnp.sin(ang_c)], axis=-1)
    return jnp.concatenate([cos64, cos64], axis=-1), jnp.concatenate([sin64, sin64], axis=-1)


def kernel(x_prompt, x_sample, cache_k, cache_v, c, c_ctx, norm1_g, norm2_g, w_ada, b_ada, attn_w_qkv, attn_w_o, attn_q_g, attn_k_g, attn_lq1, attn_lk1, attn_lq2, attn_lk2, attn_sub_g, conv_w_pw1, conv_b_pw1, conv_w_dw, conv_b_dw, conv_ln_g, conv_ln_b, conv_w_pw2, conv_b_pw2, pool_w, pool_scale, moe_w_router, moe_b_router, moe_w_gate_up, moe_b_gate_up, moe_w_down, moe_b_down):
    p = {
        "norm1_g": norm1_g, "norm2_g": norm2_g,
        "attn_w_qkv": attn_w_qkv, "attn_w_o": attn_w_o, "attn_q_g": attn_q_g, "attn_k_g": attn_k_g,
        "attn_lq1": attn_lq1, "attn_lk1": attn_lk1, "attn_lq2": attn_lq2, "attn_lk2": attn_lk2,
        "attn_sub_g": attn_sub_g,
        "conv_w_pw1": conv_w_pw1, "conv_b_pw1": conv_b_pw1, "conv_w_dw": conv_w_dw, "conv_b_dw": conv_b_dw,
        "conv_ln_g": conv_ln_g, "conv_ln_b": conv_ln_b, "conv_w_pw2": conv_w_pw2, "conv_b_pw2": conv_b_pw2,
        "pool_w": pool_w, "pool_scale": pool_scale,
        "moe_w_router": moe_w_router, "moe_b_router": moe_b_router, "moe_w_gate_up": moe_w_gate_up,
        "moe_b_gate_up": moe_b_gate_up, "moe_w_down": moe_w_down, "moe_b_down": moe_b_down,
    }
    cond = jnp.concatenate([c_ctx[None, :], c, jnp.zeros((N_COND - 1 - DEC_BATCH, D), F32)], axis=0)
    mod = _ada_table(cond, w_ada, b_ada).reshape(DEPTH, N_COND, 1, 6 * D)
    rope_tabs = _rope_tables()
    ck = cache_k.transpose(0, 1, 2, 4, 3, 5).reshape(DEC_BATCH, -1, H, PAST_LEN, 2 * HEAD_DIM)

    x_all = jnp.concatenate([x_prompt.reshape(N_P, D), x_sample.reshape(N_S, D)], axis=0)
    kv_acc = None
    for i in range(DEPTH):
        j, kind = i // N_MIXERS, i % N_MIXERS
        if kind == 0:
            x_all, *kv_acc = _attention(x_all, mod, i, j, False, p, None, None, None, kv_acc)
            (x_all,) = _attention(x_all, mod, i, j, True, p, rope_tabs, ck, cache_v)
        elif kind == 1:
            x_all = _conformer(x_all, mod, i, j, False, p)
            x_all = _conformer(x_all, mod, i, j, True, p)
        else:
            x_all = _pool_mixer(x_all, mod, i, j, False, p)
            x_all = _pool_mixer(x_all, mod, i, j, True, p)
        x_all = _moe(x_all, mod, i, p)
    y_prompt = x_all[:N_P].reshape(BATCH, SEQ, D)
    y_sample = x_all[N_P:].reshape(DEC_BATCH, DEC_SEQ, D)
    return (y_prompt, y_sample, kv_acc[0], kv_acc[1])
```

```python
import functools
import math

import jax
import jax.numpy as jnp
from jax import lax
from jax.experimental import pallas as pl
from jax.experimental.pallas import tpu as pltpu

F32 = jnp.float32
BF16 = jnp.bfloat16

D = 1024
DEPTH = 4
BATCH, SEQ = 32, 256
DEC_BATCH, DEC_SEQ = 4, 1024
PAST_LEN = 256
GRID_W = 64
N_MIXERS = 3
H = 8
HEAD_DIM = 64
V_DIM = 2 * HEAD_DIM
QK_W = H * 2 * HEAD_DIM
ROPE_THETA = 10000.0
CONV_WIDTH = 31
POOL_WINDOWS = (2, 4, 8, 16)
POOL_GROUP = D // 4
E = 32
TOP_K = 4
FF = D
SWIGLU_ALPHA = 1.702
SWIGLU_LIMIT = 7.0
RMS_EPS = 1e-6
LN_EPS = 1e-5

N_P = BATCH * SEQ
N_S = DEC_BATCH * DEC_SEQ
N_TOK = N_P + N_S
N_COND = 8

TM = 256
NK = N_TOK * TOP_K
N_TILES = NK // TM + E
R_SLOTS = N_TILES * TM
HALO = 16
ROW_BLK = 32
HEADS_PER_STEP = 2
assert TOP_K == 4
VMEM_LIMIT = 56 * 1024 * 1024


def _cparams(sem=None):
    return pltpu.CompilerParams(dimension_semantics=sem, vmem_limit_bytes=VMEM_LIMIT)


def _cond_row(tile, rows_per_tile):
    n_prompt_tiles = N_P // rows_per_tile
    tiles_per_seq = DEC_SEQ // rows_per_tile
    return jnp.where(tile < n_prompt_tiles, 0, 1 + (tile - n_prompt_tiles) // tiles_per_seq)


def _mod_spec(layer, row_of_step):
    return pl.BlockSpec((None, None, 1, 6 * D), lambda b, *_: (layer, row_of_step(b), 0, 0))


def _full(shape):
    nd = len(shape)
    return pl.BlockSpec(shape, lambda *_: (0,) * nd)


def _rms_mod(x, g, sc, sh):
    ms = jnp.mean(x * x, axis=-1, keepdims=True)
    return (x * lax.rsqrt(ms + RMS_EPS) * g) * (1.0 + sc) + sh


def _ada_body(cond_ref, w_ref, b_ref, o_ref):
    c = cond_ref[...]
    a = (c * jax.nn.sigmoid(c)).astype(BF16)
    o_ref[0] = jnp.dot(a, w_ref[0].astype(BF16), preferred_element_type=F32) + b_ref[0]


def _ada_table(cond, w_ada, b_ada):
    tn = 1536
    return pl.pallas_call(
        _ada_body,
        grid=(DEPTH, 6 * D // tn),
        in_specs=[
            pl.BlockSpec((N_COND, D), lambda i, j: (0, 0)),
            pl.BlockSpec((1, D, tn), lambda i, j: (i, 0, j)),
            pl.BlockSpec((1, 1, tn), lambda i, j: (i, 0, j)),
        ],
        out_specs=pl.BlockSpec((1, N_COND, tn), lambda i, j: (i, 0, j)),
        out_shape=jax.ShapeDtypeStruct((DEPTH, N_COND, 6 * D), F32),
        compiler_params=_cparams(("arbitrary", "arbitrary")),
        name="ada_table",
    )(cond, w_ada, b_ada.reshape(DEPTH, 1, 6 * D))


def _half_norm(x, g, lo):
    ss = x * x
    s_lo = jnp.sum(jnp.where(lo, ss, 0.0), axis=-1, keepdims=True)
    s_hi = jnp.sum(jnp.where(lo, 0.0, ss), axis=-1, keepdims=True)
    ms = jnp.where(lo, s_lo, s_hi) * (1.0 / HEAD_DIM)
    return x * lax.rsqrt(ms + RMS_EPS) * g


def _rope(x, cos, sin_signed, first16):
    partner = jnp.where(first16, pltpu.roll(x, 128 - 16, axis=1), pltpu.roll(x, 16, axis=1))
    return x * cos + partner * sin_signed


def _dot_nt(a, b):
    return lax.dot_general(a, b, (((1,), (1,)), ((), ())), preferred_element_type=F32)


def _attn_body(*refs, seq, lam_init, latent, q_blk, kv_alias):
    it = iter(refs)
    x_ref, mod_ref, n1g_ref, wqkv_ref, wo_ref, qg_ref, kg_ref = (next(it) for _ in range(7))
    lq1_ref, lk1_ref, lq2_ref, lk2_ref, subg_ref = (next(it) for _ in range(5))
    if latent:
        cos_ref, sin_ref, ck_ref, cv_ref = (next(it) for _ in range(4))
    if kv_alias:
        next(it), next(it)
    xo_ref = next(it)
    if not latent:
        ko_ref, vo_ref = next(it), next(it)
    qkv_s, qlo_s, qhi_s, kb_s, vb_s, oh_s, o_s = (next(it) for _ in range(7))

    mod = mod_ref[...]
    sh1, sc1, g1 = mod[:, 0:D], mod[:, D:2 * D], mod[:, 2 * D:3 * D]
    n_col = (2 * QK_W + H * V_DIM) // 128
    for c in range(seq // TM):
        rs = slice(c * TM, (c + 1) * TM)
        h = _rms_mod(x_ref[rs, :], n1g_ref[...], sc1, sh1).astype(BF16)
        qkv = jnp.dot(h, wqkv_ref[...], preferred_element_type=F32)
        for cb in range(n_col):
            qkv_s[cb, rs, :] = qkv[:, cb * 128:(cb + 1) * 128]

    lam = (jnp.exp(jnp.sum(lq1_ref[...] * lk1_ref[...], axis=-1, keepdims=True))
           - jnp.exp(jnp.sum(lq2_ref[...] * lk2_ref[...], axis=-1, keepdims=True)) + lam_init)

    lane = lax.broadcasted_iota(jnp.int32, (1, 2 * HEAD_DIM), 1)
    lo = lane < HEAD_DIM
    first16 = (lane % 32) < 16
    scale = 1.0 / math.sqrt(HEAD_DIM)

    def prepare(hd, u):
        qn = _half_norm(qkv_s[hd], qg_ref[...], lo)
        kn = _half_norm(qkv_s[H + hd], kg_ref[...], lo)
        vh = qkv_s[2 * H + hd]
        if latent:
            qn = _rope(qn, cos_ref[...], sin_ref[...], first16)
            kn = _rope(kn, cos_ref[...], sin_ref[...], first16)
        else:
            ko_ref[hd, 0] = kn[:, :HEAD_DIM]
            ko_ref[hd, 1] = kn[:, HEAD_DIM:]
            vo_ref[hd] = vh
        qs = qn * scale
        qlo_s[u] = jnp.where(lo, qs, 0.0).astype(BF16)
        qhi_s[u] = jnp.where(lo, 0.0, qs).astype(BF16)
        kb_s[u, 0:seq, :] = kn.astype(BF16)
        vb_s[u, 0:seq, :] = vh.astype(BF16)
        if latent:
            kb_s[u, seq:seq + PAST_LEN, :] = ck_ref[hd].astype(BF16)
            vb_s[u, seq:seq + PAST_LEN, :] = cv_ref[hd].astype(BF16)

    def head_pair(hp, carry):
        for u in range(HEADS_PER_STEP):
            prepare(HEADS_PER_STEP * hp + u, u)

        def q_block(qb, carry2):
            r0 = pl.multiple_of(qb * q_blk, q_blk)
            for u in range(HEADS_PER_STEP):
                es, dens = [], []
                for q_s in (qlo_s, qhi_s):
                    s = _dot_nt(q_s[u, pl.ds(r0, q_blk), :], kb_s[u])
                    e = jnp.exp(s - jnp.max(s, axis=-1, keepdims=True))
                    es.append(e)
                    dens.append(jnp.sum(e, axis=-1, keepdims=True))
                a = (es[0] * (1.0 / dens[0]) - es[1] * (lam / dens[1])).astype(BF16)
                o = jnp.dot(a, vb_s[u], preferred_element_type=F32)
                on = (o * lax.rsqrt(jnp.mean(o * o, axis=-1, keepdims=True) + RMS_EPS) * subg_ref[...]
                      * (1.0 - lam_init))
                oh_s[HEADS_PER_STEP * hp + u, pl.ds(r0, q_blk), :] = on.astype(BF16)
            return carry2

        lax.fori_loop(0, seq // q_blk, q_block, 0)
        return carry

    lax.fori_loop(0, H // HEADS_PER_STEP, head_pair, 0)

    for hd in range(H):
        o_s[:, hd * 128:(hd + 1) * 128] = oh_s[hd]
    for c in range(seq // TM):
        rs = slice(c * TM, (c + 1) * TM)
        out = jnp.dot(o_s[rs, :], wo_ref[...], preferred_element_type=F32)
        xo_ref[rs, :] = x_ref[rs, :] + g1 * out


def _attention(x_all, mod, layer, j, latent, p, rope_tabs, ck, cv, kv_acc=None):
    seq = DEC_SEQ if latent else SEQ
    nb = DEC_BATCH if latent else BATCH
    blk_off = N_P // seq if latent else 0
    n_keys = seq + PAST_LEN if latent else seq
    lam_init = 0.8 - 0.6 * math.exp(-0.3 * layer)
    row = (lambda b: 1 + b) if latent else (lambda b: 0)
    two = lambda v: jnp.concatenate([v, v])[None, :]
    vec = lambda v: v[None, :]

    args = [x_all, mod, vec(p["norm1_g"][layer]), p["attn_w_qkv"][j].astype(BF16), p["attn_w_o"][j].astype(BF16),
            two(p["attn_q_g"][j]), two(p["attn_k_g"][j]), vec(p["attn_lq1"][j]), vec(p["attn_lk1"][j]),
            vec(p["attn_lq2"][j]), vec(p["attn_lk2"][j]), vec(p["attn_sub_g"][j])]
    in_specs = [pl.BlockSpec((seq, D), lambda b: (blk_off + b, 0)), _mod_spec(layer, row), _full((1, D)),
                _full((D, 2 * QK_W + H * V_DIM)), _full((H * V_DIM, D)), _full((1, 128)), _full((1, 128)),
                _full((1, HEAD_DIM)), _full((1, HEAD_DIM)), _full((1, HEAD_DIM)), _full((1, HEAD_DIM)),
                _full((1, V_DIM))]
    out_shape = [jax.ShapeDtypeStruct((N_TOK, D), F32)]
    out_specs = [pl.BlockSpec((seq, D), lambda b: (blk_off + b, 0))]
    aliases = {0: 0}
    if latent:
        args += [rope_tabs[0], rope_tabs[1], ck, cv]
        in_specs += [_full((seq, 128)), _full((seq, 128)),
                     pl.BlockSpec((None, None, H, PAST_LEN, 128), lambda b: (b, j, 0, 0, 0)),
                     pl.BlockSpec((None, None, H, PAST_LEN, V_DIM), lambda b: (b, j, 0, 0, 0))]
    else:
        n_attn = (DEPTH + N_MIXERS - 1) // N_MIXERS
        out_shape += [jax.ShapeDtypeStruct((nb, n_attn, H, 2, seq, HEAD_DIM), F32),
                      jax.ShapeDtypeStruct((nb, n_attn, H, seq, V_DIM), F32)]
        out_specs += [pl.BlockSpec((None, None, H, 2, seq, HEAD_DIM), lambda b: (b, j, 0, 0, 0, 0)),
                      pl.BlockSpec((None, None, H, seq, V_DIM), lambda b: (b, j, 0, 0, 0))]
        if kv_acc is not None:
            aliases = {0: 0, len(args): 1, len(args) + 1: 2}
            args += list(kv_acc)
            in_specs += [pl.BlockSpec(memory_space=pl.ANY), pl.BlockSpec(memory_space=pl.ANY)]
    return pl.pallas_call(
        functools.partial(_attn_body, seq=seq, lam_init=lam_init, latent=latent, q_blk=256,
                          kv_alias=kv_acc is not None),
        grid=(nb,),
        in_specs=in_specs,
        out_specs=out_specs,
        out_shape=out_shape,
        scratch_shapes=[pltpu.VMEM(((2 * QK_W + H * V_DIM) // 128, seq, 128), F32),
                        pltpu.VMEM((HEADS_PER_STEP, seq, 128), BF16), pltpu.VMEM((HEADS_PER_STEP, seq, 128), BF16),
                        pltpu.VMEM((HEADS_PER_STEP, n_keys, 128), BF16),
                        pltpu.VMEM((HEADS_PER_STEP, n_keys, V_DIM), BF16),
                        pltpu.VMEM((H, seq, V_DIM), BF16), pltpu.VMEM((seq, H * V_DIM), BF16)],
        input_output_aliases=aliases,
        compiler_params=_cparams(("arbitrary",)),
        name="attn_latent" if latent else "attn_prompt",
    )(*args)


def _conv_body(x_ref, mod_ref, n1g_ref, w1_ref, b1_ref, wdw_ref, bdw_ref, lng_ref, lnb_ref, w2_ref, b2_ref,
               xo_ref, upad, cv_s, *, seq):
    mod = mod_ref[...]
    sh1, sc1, g1 = mod[:, 0:D], mod[:, D:2 * D], mod[:, 2 * D:3 * D]
    zeros = jnp.zeros((HALO, D), F32)
    upad[0:HALO, :] = zeros
    upad[HALO + seq:2 * HALO + seq, :] = zeros
    for c in range(seq // TM):
        rs = slice(c * TM, (c + 1) * TM)
        h = _rms_mod(x_ref[rs, :], n1g_ref[...], sc1, sh1).astype(BF16)
        y = jnp.dot(h, w1_ref[...], preferred_element_type=F32) + b1_ref[...]
        upad[HALO + c * TM:HALO + (c + 1) * TM, :] = y[:, :D] * jax.nn.sigmoid(y[:, D:])

    pad = CONV_WIDTH // 2
    cb = 256

    def conv_rows(rb, carry):
        r0 = pl.multiple_of(rb * ROW_BLK, ROW_BLK)
        for c in range(D // cb):
            cs = slice(c * cb, (c + 1) * cb)
            win = upad[pl.ds(r0, ROW_BLK + 2 * HALO), cs]
            acc = jnp.zeros((ROW_BLK, cb), F32) + bdw_ref[:, cs]
            for k in range(CONV_WIDTH):
                off = HALO - pad + k
                acc = acc + win[off:off + ROW_BLK, :] * wdw_ref[k:k + 1, cs]
            cv_s[pl.ds(r0, ROW_BLK), cs] = acc
        return carry

    lax.fori_loop(0, seq // ROW_BLK, conv_rows, 0)

    for c in range(seq // TM):
        rs = slice(c * TM, (c + 1) * TM)
        v = cv_s[rs, :]
        mu = jnp.mean(v, axis=-1, keepdims=True)
        var = jnp.mean(jnp.square(v - mu), axis=-1, keepdims=True)
        yn = (v - mu) * lax.rsqrt(var + LN_EPS) * lng_ref[...] + lnb_ref[...]
        u2 = (yn * jax.nn.sigmoid(yn)).astype(BF16)
        out = jnp.dot(u2, w2_ref[...], preferred_element_type=F32) + b2_ref[...]
        xo_ref[rs, :] = x_ref[rs, :] + g1 * out


def _conformer(x_all, mod, layer, j, latent, p):
    seq = DEC_SEQ if latent else SEQ
    nb = DEC_BATCH if latent else BATCH
    blk_off = N_P // seq if latent else 0
    row = (lambda b: 1 + b) if latent else (lambda b: 0)
    vec = lambda v: v[None, :]
    return pl.pallas_call(
        functools.partial(_conv_body, seq=seq),
        grid=(nb,),
        in_specs=[pl.BlockSpec((seq, D), lambda b: (blk_off + b, 0)), _mod_spec(layer, row), _full((1, D)),
                  _full((D, 2 * D)), _full((1, 2 * D)), _full((CONV_WIDTH, D)), _full((1, D)), _full((1, D)),
                  _full((1, D)), _full((D, D)), _full((1, D))],
        out_specs=pl.BlockSpec((seq, D), lambda b: (blk_off + b, 0)),
        out_shape=jax.ShapeDtypeStruct((N_TOK, D), F32),
        scratch_shapes=[pltpu.VMEM((seq + 2 * HALO, D), F32), pltpu.VMEM((seq, D), F32)],
        input_output_aliases={0: 0},
        compiler_params=_cparams(("arbitrary",)),
        name="conv_latent" if latent else "conv_prompt",
    )(x_all, mod, vec(p["norm1_g"][layer]), p["conv_w_pw1"][j].astype(BF16), vec(p["conv_b_pw1"][j]),
      p["conv_w_dw"][j], vec(p["conv_b_dw"][j]), vec(p["conv_ln_g"][j]), vec(p["conv_ln_b"][j]),
      p["conv_w_pw2"][j].astype(BF16), vec(p["conv_b_pw2"][j]))


def _pool_body(x_ref, mod_ref, n1g_ref, wp_ref, ps_ref, xo_ref, hpad, p_s, *, seq):
    mod = mod_ref[...]
    sh1, sc1, g1 = mod[:, 0:D], mod[:, D:2 * D], mod[:, 2 * D:3 * D]
    zeros = jnp.zeros((HALO, D), F32)
    hpad[0:HALO, :] = zeros
    hpad[HALO + seq:2 * HALO + seq, :] = zeros
    for c in range(seq // TM):
        rs = slice(c * TM, (c + 1) * TM)
        hpad[HALO + c * TM:HALO + (c + 1) * TM, :] = _rms_mod(x_ref[rs, :], n1g_ref[...], sc1, sh1)

    def pool_rows(rb, carry):
        r0 = pl.multiple_of(rb * ROW_BLK, ROW_BLK)
        t = r0 + lax.broadcasted_iota(jnp.int32, (ROW_BLK, 1), 0)
        for g, w in enumerate(POOL_WINDOWS):
            cs = slice(g * POOL_GROUP, (g + 1) * POOL_GROUP)
            win = hpad[pl.ds(r0, ROW_BLK + 2 * HALO), cs]
            acc = jnp.zeros((ROW_BLK, POOL_GROUP), F32)
            for d in range(-(w // 2), w - w // 2):
                acc = acc + win[HALO + d:HALO + d + ROW_BLK, :]
            cnt = jnp.minimum(t + (w - w // 2), seq) - jnp.maximum(t - w // 2, 0)
            pooled = acc / cnt.astype(F32) - win[HALO:HALO + ROW_BLK, :]
            p_s[pl.ds(r0, ROW_BLK), cs] = pooled.astype(BF16)
        return carry

    lax.fori_loop(0, seq // ROW_BLK, pool_rows, 0)

    for c in range(seq // TM):
        rs = slice(c * TM, (c + 1) * TM)
        for g in range(len(POOL_WINDOWS)):
            cs = slice(g * POOL_GROUP, (g + 1) * POOL_GROUP)
            y = jnp.dot(p_s[rs, cs], wp_ref[g], preferred_element_type=F32) * ps_ref[:, cs]
            xo_ref[rs, cs] = x_ref[rs, cs] + g1[:, cs] * y


def _pool_mixer(x_all, mod, layer, j, latent, p):
    seq = DEC_SEQ if latent else SEQ
    nb = DEC_BATCH if latent else BATCH
    blk_off = N_P // seq if latent else 0
    row = (lambda b: 1 + b) if latent else (lambda b: 0)
    ng = len(POOL_WINDOWS)
    return pl.pallas_call(
        functools.partial(_pool_body, seq=seq),
        grid=(nb,),
        in_specs=[pl.BlockSpec((seq, D), lambda b: (blk_off + b, 0)), _mod_spec(layer, row), _full((1, D)),
                  _full((ng, POOL_GROUP, POOL_GROUP)), _full((1, D))],
        out_specs=pl.BlockSpec((seq, D), lambda b: (blk_off + b, 0)),
        out_shape=jax.ShapeDtypeStruct((N_TOK, D), F32),
        scratch_shapes=[pltpu.VMEM((seq + 2 * HALO, D), F32), pltpu.VMEM((seq, D), BF16)],
        input_output_aliases={0: 0},
        compiler_params=_cparams(("arbitrary",)),
        name="pool_latent" if latent else "pool_prompt",
    )(x_all, mod, p["norm1_g"][layer][None, :], p["pool_w"][j].astype(BF16), p["pool_scale"][j][None, :])


def _moe_input(x_ref, mod_ref, n2g_ref):
    mod = mod_ref[...]
    return _rms_mod(x_ref[...], n2g_ref[...], mod[:, 4 * D:5 * D], mod[:, 3 * D:4 * D])


def _router_body(x_ref, mod_ref, n2g_ref, wr_ref, br_ref, idx_ref, gate_ref, cnt_ref, cnt_s):
    t = pl.program_id(0)

    @pl.when(t == 0)
    def _():
        cnt_s[...] = jnp.zeros_like(cnt_s)

    h = _moe_input(x_ref, mod_ref, n2g_ref)
    logits = jnp.dot(h.astype(BF16), wr_ref[...], preferred_element_type=F32) + br_ref[...]
    lane = lax.broadcasted_iota(jnp.int32, logits.shape, 1)
    neg_inf = float("-inf")
    vals, idxs = [], []
    for _ in range(TOP_K):
        m = jnp.max(logits, axis=-1, keepdims=True)
        ix = jnp.min(jnp.where(logits == m, lane, 128), axis=-1, keepdims=True)
        vals.append(m)
        idxs.append(ix)
        logits = jnp.where(lane == ix, neg_inf, logits)
    es = [jnp.exp(v - vals[0]) for v in vals]
    den = es[0] + es[1] + es[2] + es[3]

    member = jnp.zeros(lane.shape, F32)
    for k in range(TOP_K):
        member = jnp.where(lane == idxs[k], 1.0, member)
    r_i = lax.broadcasted_iota(jnp.int32, (TM, TM), 0)
    c_i = lax.broadcasted_iota(jnp.int32, (TM, TM), 1)
    earlier = jnp.where(c_i < r_i, 1.0, 0.0).astype(BF16)
    before = jnp.dot(earlier, member.astype(BF16), preferred_element_type=F32) + cnt_s[...]
    cnt_s[...] = cnt_s[...] + jnp.sum(member, axis=0, keepdims=True)
    cnt_ref[...] = cnt_s[...].astype(jnp.int32)

    idx_out = jnp.zeros(lane.shape, jnp.int32)
    gate_out = jnp.zeros(lane.shape, F32)
    for k in range(TOP_K):
        rank_k = jnp.sum(jnp.where(lane == idxs[k], before, 0.0), axis=-1, keepdims=True).astype(jnp.int32)
        idx_out = jnp.where(lane == k, idxs[k], idx_out)
        idx_out = jnp.where(lane == TOP_K + k, rank_k, idx_out)
        gate_out = jnp.where(lane == k, es[k] / den, gate_out)
    idx_ref[...] = idx_out
    gate_ref[...] = gate_out


def _router(x_all, mod, layer, p):
    wr = jnp.zeros((D, 128), BF16).at[:, :E].set(p["moe_w_router"][layer].astype(BF16))
    br = jnp.full((1, 128), float("-inf"), F32).at[0, :E].set(p["moe_b_router"][layer])
    row = lambda t: _cond_row(t, TM)
    return pl.pallas_call(
        _router_body,
        grid=(N_TOK // TM,),
        in_specs=[pl.BlockSpec((TM, D), lambda t: (t, 0)), _mod_spec(layer, row), _full((1, D)),
                  _full((D, 128)), _full((1, 128))],
        out_specs=[pl.BlockSpec((TM, 128), lambda t: (t, 0)),
                   pl.BlockSpec((TM, 128), lambda t: (t, 0)), pl.BlockSpec((1, 128), lambda t: (0, 0))],
        out_shape=[jax.ShapeDtypeStruct((N_TOK, 128), jnp.int32),
                   jax.ShapeDtypeStruct((N_TOK, 128), F32), jax.ShapeDtypeStruct((1, 128), jnp.int32)],
        scratch_shapes=[pltpu.VMEM((1, 128), F32)],
        compiler_params=_cparams(("arbitrary",)),
        name="router",
    )(x_all, mod, p["norm2_g"][layer][None, :], wr, br)


def _dispatch_plan(idx_rank, counts):
    counts = counts[0, :E]
    padded = ((counts + TM - 1) // TM) * TM
    pad_end = jnp.cumsum(padded)
    pad_start = pad_end - padded
    top_idx, rank = idx_rank[:, :TOP_K], idx_rank[:, TOP_K:2 * TOP_K]
    onehot = top_idx[:, :, None] == jnp.arange(E, dtype=jnp.int32)[None, None, :]
    dest = (jnp.sum(jnp.where(onehot, pad_start[None, None, :], 0), axis=-1) + rank).reshape(NK)
    as_i32 = lambda v: v.astype(jnp.int32)
    return as_i32(dest), as_i32(pad_start // TM), as_i32(padded // TM), as_i32(pad_start + counts), as_i32(pad_end)


def _tile_rows(row):
    return pl.ds(row * 8 if isinstance(row, int) else pl.multiple_of(row * 8, 8), 8)


def _dispatch_body(dest_ref, fill_lo_ref, fill_hi_ref, x_ref, mod_ref, n2g_ref, xs_hbm, h_ref, sem, fill_sem):
    i = pl.program_id(0)
    base = i * (TM * TOP_K)
    h = _moe_input(x_ref, mod_ref, n2g_ref)
    for s in range(D // 128):
        h_ref[pl.ds(s, TM, stride=8), :] = h[:, s * 128:(s + 1) * 128]

    def token(t, carry):
        src = h_ref.at[_tile_rows(t)]
        for k in range(TOP_K):
            pltpu.make_async_copy(src, xs_hbm.at[_tile_rows(dest_ref[base + t * TOP_K + k])], sem).start(priority=k % 2)
        return carry

    lax.fori_loop(0, TM, token, 0, unroll=8)

    @pl.when(i == 0)
    def _():
        def fill_expert(e, carry):
            def start(slot, c2):
                pltpu.make_async_copy(h_ref.at[_tile_rows(0)], xs_hbm.at[_tile_rows(slot)], fill_sem).start()
                return c2

            def wait(slot, c2):
                pltpu.make_async_copy(h_ref.at[_tile_rows(0)], xs_hbm.at[_tile_rows(slot)], fill_sem).wait()
                return c2

            lax.fori_loop(fill_lo_ref[e], fill_hi_ref[e], start, 0)
            lax.fori_loop(fill_lo_ref[e], fill_hi_ref[e], wait, 0)
            return carry

        lax.fori_loop(0, E, fill_expert, 0)

    n = TM * TOP_K * 8
    pltpu.make_async_copy(xs_hbm.at[pl.ds(0, n)], xs_hbm.at[pl.ds(0, n)], sem).wait()


def _dispatch(dest, fill_lo, fill_hi, x_all, mod, layer, p):
    row = lambda t: _cond_row(t, TM)
    return pl.pallas_call(
        _dispatch_body,
        grid_spec=pltpu.PrefetchScalarGridSpec(
            num_scalar_prefetch=3,
            grid=(N_TOK // TM,),
            in_specs=[pl.BlockSpec((TM, D), lambda i, *_: (i, 0)), _mod_spec(layer, row), _full((1, D))],
            out_specs=pl.BlockSpec(memory_space=pl.ANY),
            scratch_shapes=[pltpu.VMEM((TM * 8, 128), F32), pltpu.SemaphoreType.DMA, pltpu.SemaphoreType.DMA],
        ),
        out_shape=jax.ShapeDtypeStruct((R_SLOTS * 8, 128), F32),
        compiler_params=_cparams(("arbitrary",)),
        name="dispatch_rows",
    )(dest, fill_lo, fill_hi, x_all, mod, p["norm2_g"][layer][None, :])


W_CHUNKS = 8


def _expert_body(first_ref, ntiles_ref, xs_hbm, wgu_hbm, bgu_ref, wdn_hbm, bdn_ref, ys_hbm,
                 wgu_f, wdn_f, wgu_bf, wdn_bf, x_s, y_s, w_sem, in_sem, out_sem, *, layer):
    e = pl.program_id(0)
    w_slot = e % 2
    t0 = first_ref[e]
    nt = ntiles_ref[e]
    n_total = first_ref[E - 1] + ntiles_ref[E - 1]

    def weight_copies(expert, slot):
        gu_rows, dn_rows = D // W_CHUNKS, 2 * FF // W_CHUNKS
        cps = [pltpu.make_async_copy(wgu_hbm.at[layer, expert, pl.ds(c * gu_rows, gu_rows), :],
                                     wgu_f.at[slot, pl.ds(c * gu_rows, gu_rows), :], w_sem.at[slot])
               for c in range(W_CHUNKS)]
        cps += [pltpu.make_async_copy(wdn_hbm.at[layer, expert, pl.ds(c * dn_rows, dn_rows), :],
                                      wdn_f.at[slot, pl.ds(c * dn_rows, dn_rows), :], w_sem.at[slot])
                for c in range(FF // dn_rows)]
        return cps

    def start_weights(expert, slot):
        for cp in weight_copies(expert, slot):
            cp.start(priority=1)

    half = TM * 8 // 2

    def x_copies(tile, slot):
        return [pltpu.make_async_copy(xs_hbm.at[pl.ds(pl.multiple_of(tile * (TM * 8) + h * half, half), half)],
                                      x_s.at[slot, pl.ds(h * half, half)], in_sem.at[slot]) for h in range(2)]

    def y_copies(tile, slot):
        return [pltpu.make_async_copy(y_s.at[slot, pl.ds(h * half, half)],
                                      ys_hbm.at[pl.ds(pl.multiple_of(tile * (TM * 8) + h * half, half), half)],
                                      out_sem.at[slot]) for h in range(2)]

    def start_all(copies):
        for cp in copies:
            cp.start(priority=0)

    def wait_all(copies):
        for cp in copies:
            cp.wait()

    @pl.when(e == 0)
    def _():
        start_weights(0, 0)

    @pl.when(e + 1 < E)
    def _():
        start_weights(e + 1, 1 - w_slot)

    @pl.when(jnp.logical_and(nt > 0, t0 == 0))
    def _():
        start_all(x_copies(0, 0))

    wait_all(weight_copies(e, w_slot))
    for c in range(D // TM):
        rs = slice(c * TM, (c + 1) * TM)
        wgu_bf[rs, :] = wgu_f[w_slot, rs, :].astype(BF16)
        wdn_bf[rs, :] = wdn_f[w_slot, rs, :].astype(BF16)

    def tile_step(j, carry):
        g = t0 + j
        slot = g % 2

        @pl.when(g + 1 < n_total)
        def _():
            start_all(x_copies(g + 1, 1 - slot))

        wait_all(x_copies(g, slot))

        @pl.when(g >= 2)
        def _():
            wait_all(y_copies(g - 2, slot))

        x = jnp.concatenate([x_s[slot, pl.ds(s, TM, stride=8), :].astype(BF16) for s in range(D // 128)], axis=1)
        gu = jnp.dot(x, wgu_bf[...], preferred_element_type=F32) + bgu_ref[0]
        gate = jnp.minimum(gu[:, :FF], SWIGLU_LIMIT)
        up = jnp.clip(gu[:, FF:], -SWIGLU_LIMIT, SWIGLU_LIMIT)
        glu = gate * jax.nn.sigmoid(SWIGLU_ALPHA * gate)
        act = ((up + 1.0) * glu).astype(BF16)
        y = jnp.dot(act, wdn_bf[...], preferred_element_type=F32) + bdn_ref[0]
        for s in range(D // 128):
            y_s[slot, pl.ds(s, TM, stride=8), :] = y[:, s * 128:(s + 1) * 128]
        start_all(y_copies(g, slot))
        return carry

    lax.fori_loop(0, nt, tile_step, 0)

    @pl.when(e == E - 1)
    def _():
        @pl.when(n_total >= 2)
        def _():
            wait_all(y_copies(n_total - 2, n_total % 2))

        @pl.when(n_total >= 1)
        def _():
            wait_all(y_copies(n_total - 1, (n_total - 1) % 2))


def _experts(first_tile, n_tiles, xs, layer, p):
    return pl.pallas_call(
        functools.partial(_expert_body, layer=layer),
        grid_spec=pltpu.PrefetchScalarGridSpec(
            num_scalar_prefetch=2,
            grid=(E,),
            in_specs=[pl.BlockSpec(memory_space=pl.ANY),
                      pl.BlockSpec(memory_space=pl.ANY),
                      pl.BlockSpec((None, 1, 1, 2 * FF), lambda e, *_: (layer, e, 0, 0)),
                      pl.BlockSpec(memory_space=pl.ANY),
                      pl.BlockSpec((None, 1, 1, D), lambda e, *_: (layer, e, 0, 0))],
            out_specs=pl.BlockSpec(memory_space=pl.ANY),
            scratch_shapes=[pltpu.VMEM((2, D, 2 * FF), F32), pltpu.VMEM((2, FF, D), F32),
                            pltpu.VMEM((D, 2 * FF), BF16), pltpu.VMEM((FF, D), BF16),
                            pltpu.VMEM((2, TM * 8, 128), F32), pltpu.VMEM((2, TM * 8, 128), F32),
                            pltpu.SemaphoreType.DMA((2,)), pltpu.SemaphoreType.DMA((2,)),
                            pltpu.SemaphoreType.DMA((2,))],
        ),
        out_shape=jax.ShapeDtypeStruct((R_SLOTS * 8, 128), F32),
        compiler_params=_cparams(("arbitrary",)),
        name="experts",
    )(first_tile, n_tiles, xs, p["moe_w_gate_up"], p["moe_b_gate_up"].reshape(DEPTH, E, 1, 2 * FF),
      p["moe_w_down"], p["moe_b_down"].reshape(DEPTH, E, 1, D))


def _combine_body(dest_ref, x_ref, mod_ref, gate_ref, ys_hbm, xo_ref, yg_s, sem):
    i = pl.program_id(0)

    def fetch(tile, slot):
        base = tile * (TM * TOP_K)

        def token(t, carry):
            for k in range(TOP_K):
                pltpu.make_async_copy(ys_hbm.at[_tile_rows(dest_ref[base + t * TOP_K + k])],
                                      yg_s.at[slot, k, _tile_rows(t)], sem.at[slot]).start(priority=k % 2)
            return carry

        lax.fori_loop(0, TM, token, 0, unroll=8)

    @pl.when(i == 0)
    def _():
        fetch(0, 0)

    @pl.when(i + 1 < pl.num_programs(0))
    def _():
        fetch(i + 1, (i + 1) % 2)

    slot = i % 2
    pltpu.make_async_copy(yg_s.at[slot], yg_s.at[slot], sem.at[slot]).wait()
    gates = gate_ref[...]
    for s in range(D // 128):
        cs = slice(s * 128, (s + 1) * 128)
        f = gates[:, 0:1] * yg_s[slot, 0, pl.ds(s, TM, stride=8), :]
        for k in range(1, TOP_K):
            f = f + gates[:, k:k + 1] * yg_s[slot, k, pl.ds(s, TM, stride=8), :]
        xo_ref[:, cs] = x_ref[:, cs] + mod_ref[:, 5 * D + s * 128:5 * D + (s + 1) * 128] * f


def _combine(x_all, mod, layer, dest, ys, gates):
    row = lambda t: _cond_row(t, TM)
    return pl.pallas_call(
        _combine_body,
        grid_spec=pltpu.PrefetchScalarGridSpec(
            num_scalar_prefetch=1,
            grid=(N_TOK // TM,),
            in_specs=[pl.BlockSpec((TM, D), lambda t, *_: (t, 0)), _mod_spec(layer, row),
                      pl.BlockSpec((TM, 128), lambda t, *_: (t, 0)), pl.BlockSpec(memory_space=pl.ANY)],
            out_specs=pl.BlockSpec((TM, D), lambda t, *_: (t, 0)),
            scratch_shapes=[pltpu.VMEM((2, TOP_K, TM * 8, 128), F32), pltpu.SemaphoreType.DMA((2,))],
        ),
        out_shape=jax.ShapeDtypeStruct((N_TOK, D), F32),
        input_output_aliases={1: 0},
        compiler_params=_cparams(("arbitrary",)),
        name="combine",
    )(dest, x_all, mod, gates, ys)


def _moe(x_all, mod, layer, p):
    idx_rank, gates, counts = _router(x_all, mod, layer, p)
    dest, first_tile, n_tiles, fill_lo, fill_hi = _dispatch_plan(idx_rank, counts)
    xs = _dispatch(dest, fill_lo, fill_hi, x_all, mod, layer, p)
    ys = _experts(first_tile, n_tiles, xs, layer, p)
    return _combine(x_all, mod, layer, dest, ys, gates)


def _rope_tables():
    rows = DEC_SEQ // GRID_W
    r = jnp.repeat(jnp.arange(rows), GRID_W).astype(F32)
    col = jnp.tile(jnp.arange(GRID_W), rows).astype(F32)
    half = HEAD_DIM // 2
    inv = ROPE_THETA ** (-jnp.arange(0, half, 2, dtype=F32) / half)
    ang_r = r[:, None] * inv[None, :]
    ang_c = col[:, None] * inv[None, :]
    cos64 = jnp.concatenate([jnp.cos(ang_r), jnp.cos(ang_r), jnp.cos(ang_c), jnp.cos(ang_c)], axis=-1)
    sin64 = jnp.concatenate([-jnp.sin(ang_r), jnp.sin(ang_r), -jnp.sin(ang_c), jnp.sin(ang_c)], axis=-1)
    return jnp.concatenate([cos64, cos64], axis=-1), jnp.concatenate([sin64, sin64], axis=-1)


def kernel(x_prompt, x_sample, cache_k, cache_v, c, c_ctx, norm1_g, norm2_g, w_ada, b_ada, attn_w_qkv, attn_w_o, attn_q_g, attn_k_g, attn_lq1, attn_lk1, attn_lq2, attn_lk2, attn_sub_g, conv_w_pw1, conv_b_pw1, conv_w_dw, conv_b_dw, conv_ln_g, conv_ln_b, conv_w_pw2, conv_b_pw2, pool_w, pool_scale, moe_w_router, moe_b_router, moe_w_gate_up, moe_b_gate_up, moe_w_down, moe_b_down):
    p = {
        "norm1_g": norm1_g, "norm2_g": norm2_g,
        "attn_w_qkv": attn_w_qkv, "attn_w_o": attn_w_o, "attn_q_g": attn_q_g, "attn_k_g": attn_k_g,
        "attn_lq1": attn_lq1, "attn_lk1": attn_lk1, "attn_lq2": attn_lq2, "attn_lk2": attn_lk2,
        "attn_sub_g": attn_sub_g,
        "conv_w_pw1": conv_w_pw1, "conv_b_pw1": conv_b_pw1, "conv_w_dw": conv_w_dw, "conv_b_dw": conv_b_dw,
        "conv_ln_g": conv_ln_g, "conv_ln_b": conv_ln_b, "conv_w_pw2": conv_w_pw2, "conv_b_pw2": conv_b_pw2,
        "pool_w": pool_w, "pool_scale": pool_scale,
        "moe_w_router": moe_w_router, "moe_b_router": moe_b_router, "moe_w_gate_up": moe_w_gate_up,
        "moe_b_gate_up": moe_b_gate_up, "moe_w_down": moe_w_down, "moe_b_down": moe_b_down,
    }
    cond = jnp.concatenate([c_ctx[None, :], c, jnp.zeros((N_COND - 1 - DEC_BATCH, D), F32)], axis=0)
    mod = _ada_table(cond, w_ada, b_ada).reshape(DEPTH, N_COND, 1, 6 * D)
    rope_tabs = _rope_tables()
    ck = cache_k.transpose(0, 1, 2, 4, 3, 5).reshape(DEC_BATCH, -1, H, PAST_LEN, 2 * HEAD_DIM)

    x_all = jnp.concatenate([x_prompt.reshape(N_P, D), x_sample.reshape(N_S, D)], axis=0)
    kv_acc = None
    for i in range(DEPTH):
        j, kind = i // N_MIXERS, i % N_MIXERS
        if kind == 0:
            x_all, *kv_acc = _attention(x_all, mod, i, j, False, p, None, None, None, kv_acc)
            (x_all,) = _attention(x_all, mod, i, j, True, p, rope_tabs, ck, cache_v)
        elif kind == 1:
            x_all = _conformer(x_all, mod, i, j, False, p)
            x_all = _conformer(x_all, mod, i, j, True, p)
        else:
            x_all = _pool_mixer(x_all, mod, i, j, False, p)
            x_all = _pool_mixer(x_all, mod, i, j, True, p)
        x_all = _moe(x_all, mod, i, p)
    y_prompt = x_all[:N_P].reshape(BATCH, SEQ, D)
    y_sample = x_all[N_P:].reshape(DEC_BATCH, DEC_SEQ, D)
    return (y_prompt, y_sample, kv_acc[0], kv_acc[1])
```
